```python
import math
import jax
import jax.numpy as jnp
from jax import lax
import numpy as np

D_MODEL = 1024
BATCH = 8
SEQ = 4096
DEPTH = 2

GRID_W = 64
CTX_LEN = 256
ROPE_BASE = 10000.0
Q_BLOCK = 128
EPS = 1e-6

DIFF_HEADS = 4
DIFF_DH = 64
DIFF_VD = 2 * DIFF_DH
DIFF_QK = DIFF_HEADS * 2 * DIFF_DH
DIFF_V = DIFF_HEADS * DIFF_VD

SSD_HEADS = 8
SSD_P = 64
SSD_INNER = SSD_HEADS * SSD_P
SSD_GROUPS = 2
SSD_STATE = 128
SSD_CONV = 5
SSD_CHUNK = 128
SSD_CONV_CH = SSD_INNER + 2 * SSD_GROUPS * SSD_STATE

MLA_HEADS = 8
MLA_NOPE = 64
MLA_ROPE = 32
MLA_V = 64
MLA_Q_LORA = 384
MLA_KV_LORA = 256
MLA_QK = MLA_NOPE + MLA_ROPE

N_BRANCH = 3
BRANCH_W = DIFF_V

IN_SIZES = (DIFF_QK, DIFF_QK, DIFF_V, SSD_INNER, SSD_CONV_CH, 2 * SSD_HEADS,
            MLA_Q_LORA, MLA_KV_LORA, MLA_ROPE, N_BRANCH * D_MODEL)
D_IN = sum(IN_SIZES)

MOE_GROUPS = 4
MOE_PER_GROUP = 8
MOE_EXPERTS = MOE_GROUPS * MOE_PER_GROUP
MOE_TOP_K = 2
MOE_HIDDEN = 256
MOE_BLOCK = 128

kernel_name = 'hybrid_diffusion_trunk'

F32 = jnp.float32


def _split_points(sizes):
    pts, acc = [], 0
    for s in sizes[:-1]:
        acc += s
        pts.append(acc)
    return pts


def rms_norm(x, g):
    xf = x.astype(F32)
    y = xf * lax.rsqrt(jnp.mean(xf * xf, axis=-1, keepdims=True) + EPS)
    return y.astype(x.dtype) * g


def modulate(h, shift, scale):
    return h * (1.0 + scale) + shift


def axial_rope_tables(rows, dim):
    quarter = dim // 4
    inv_freq = ROPE_BASE ** (-jnp.arange(quarter, dtype=F32) / quarter)
    r = jnp.repeat(jnp.arange(rows, dtype=F32), GRID_W)
    cc = jnp.tile(jnp.arange(GRID_W, dtype=F32), rows)
    ar = r[:, None] * inv_freq
    ac = cc[:, None] * inv_freq
    ang = jnp.concatenate([ar, ar, ac, ac], axis=-1)
    return jnp.cos(ang), jnp.sin(ang)


def apply_axial_rope(x, cos, sin):
    x1, x2, x3, x4 = jnp.split(x, 4, axis=-1)
    rot = jnp.concatenate([-x2, x1, -x4, x3], axis=-1)
    shape = (1, cos.shape[0]) + (1,) * (x.ndim - 3) + (cos.shape[-1],)
    return x * cos.reshape(shape).astype(x.dtype) + rot * sin.reshape(shape).astype(x.dtype)


def sweep_query_blocks(fn, q):
    bsz, L = q.shape[:2]
    qb = jnp.moveaxis(q.reshape((bsz, L // Q_BLOCK, Q_BLOCK) + q.shape[2:]), 1, 0)
    out = jnp.moveaxis(lax.map(fn, qb), 0, 1)
    return out.reshape((bsz, L) + out.shape[3:])


def diff_attention(q, k, v, lam):
    s = jnp.einsum('bqhcd,bkhcd->bhcqk', q, k).astype(F32) * (DIFF_DH ** -0.5)
    a = jax.nn.softmax(s, axis=-1)
    w = a[:, :, 0] - lam * a[:, :, 1]
    return jnp.einsum('bhqk,bkhd->bqhd', w.astype(v.dtype), v)


def softmax_attention(q, k, v):
    s = jnp.einsum('bqhd,bkhd->bhqk', q, k).astype(F32) * (MLA_QK ** -0.5)
    a = jax.nn.softmax(s, axis=-1)
    return jnp.einsum('bhqk,bkhd->bqhd', a.astype(v.dtype), v)


def centred_depthwise_conv(u, w, b):
    k = w.shape[0]
    out = lax.conv_general_dilated(u, w[:, None, :].astype(u.dtype), window_strides=(1,),
                                   padding=[((k - 1) // 2, k // 2)],
                                   dimension_numbers=('NWC', 'WIO', 'NWC'),
                                   feature_group_count=u.shape[-1])
    return out + b


def ssd_chunked_scan(xs, dt, A, Bm, Cm, h0):
    bsz, L, H, P = xs.shape
    G, N = Bm.shape[2:]
    R = H // G
    nc = L // SSD_CHUNK
    x = xs.reshape(bsz, nc, SSD_CHUNK, G, R, P)
    Bc = Bm.reshape(bsz, nc, SSD_CHUNK, G, N)
    Cc = Cm.reshape(bsz, nc, SSD_CHUNK, G, N)
    dtc = dt.reshape(bsz, nc, SSD_CHUNK, G, R)
    a_cum = jnp.cumsum(dtc * A.reshape(G, R), axis=2)
    seg = a_cum[:, :, :, None] - a_cum[:, :, None, :]
    lower = jnp.tril(jnp.ones((SSD_CHUNK, SSD_CHUNK), bool))[:, :, None, None]
    decay = jnp.exp(jnp.where(lower, seg, -jnp.inf))
    cb = jnp.einsum('bcign,bcjgn->bcijg', Cc, Bc)
    mix = cb[..., None] * decay * dtc[:, :, None]
    y_diag = jnp.einsum('bcijgr,bcjgrp->bcigrp', mix, x)
    w_end = jnp.exp(a_cum[:, :, -1:] - a_cum) * dtc
    states = jnp.einsum('bcjgn,bcjgrp->bcgrpn', Bc, x * w_end[..., None])
    chunk_decay = jnp.exp(a_cum[:, :, -1])

    def step(h, inp):
        s_c, d_c = inp
        return h * d_c[..., None, None] + s_c, h

    h_last, h_in = lax.scan(step, h0, (jnp.moveaxis(states, 1, 0), jnp.moveaxis(chunk_decay, 1, 0)))
    h_in = jnp.moveaxis(h_in, 0, 1)
    y_off = jnp.einsum('bcign,bcgrpn->bcigrp', Cc, h_in) * jnp.exp(a_cum)[..., None]
    return (y_diag + y_off).reshape(bsz, L, H, P), h_last


def branch_inputs(h, p, rope_diff, rope_mla):
    bsz, L = h.shape[:2]
    dq, dk, dv, z, xbc, dt, cq, ckv, kr, gates = jnp.split(h @ p['w_in'], _split_points(IN_SIZES), axis=-1)
    dq = rms_norm(dq.reshape(bsz, L, DIFF_HEADS, 2, DIFF_DH), p['diff_q_g'])
    dk = rms_norm(dk.reshape(bsz, L, DIFF_HEADS, 2, DIFF_DH), p['diff_k_g'])
    dv = dv.reshape(bsz, L, DIFF_HEADS, DIFF_VD)
    q = (rms_norm(cq, p['mla_cq_g']) @ p['w_uq']).reshape(bsz, L, MLA_HEADS, MLA_QK)
    kv = (rms_norm(ckv, p['mla_ckv_g']) @ p['w_ukv']).reshape(bsz, L, MLA_HEADS, MLA_NOPE + MLA_V)
    gq, gk = p['mla_q_g'], p['mla_k_g']
    q_nope = rms_norm(q[..., :MLA_NOPE], gq[:MLA_NOPE])
    q_rope = rms_norm(q[..., MLA_NOPE:], gq[MLA_NOPE:])
    k_nope = rms_norm(kv[..., :MLA_NOPE], gk[:MLA_NOPE])
    k_rope = rms_norm(kr, gk[MLA_NOPE:])
    mv = kv[..., MLA_NOPE:]
    if rope_diff is not None:
        dq = apply_axial_rope(dq, *rope_diff)
        dk = apply_axial_rope(dk, *rope_diff)
        q_rope = apply_axial_rope(q_rope, *rope_mla)
        k_rope = apply_axial_rope(k_rope, *rope_mla)
    mq = jnp.concatenate([q_nope, q_rope], axis=-1)
    mk = jnp.concatenate([k_nope, jnp.broadcast_to(k_rope[:, :, None, :], (bsz, L, MLA_HEADS, MLA_ROPE))], axis=-1)
    return {'dq': dq, 'dk': dk, 'dv': dv, 'z': z, 'xbc': xbc, 'dt': dt,
            'mq': mq, 'mk': mk, 'mv': mv, 'gates': gates}


def ssd_mixer(lat, ctx, p, need_ctx):
    A = -jnp.exp(p['ssd_A_log'].astype(F32))
    d_skip = p['ssd_D'].astype(F32)[:, None]

    def prep(s):
        u = jax.nn.silu(centred_depthwise_conv(s['xbc'], p['ssd_conv_w'], p['ssd_conv_b']))
        bsz, L = u.shape[:2]
        xs, Bm, Cm = jnp.split(u, [SSD_INNER, SSD_INNER + SSD_GROUPS * SSD_STATE], axis=-1)
        xs = xs.reshape(bsz, L, SSD_HEADS, SSD_P)
        Bm = Bm.reshape(bsz, L, SSD_GROUPS, SSD_STATE)
        Cm = Cm.reshape(bsz, L, SSD_GROUPS, SSD_STATE)
        dt = jax.nn.softplus(s['dt'].astype(F32).reshape(bsz, L, 2, SSD_HEADS) + p['ssd_dt_bias'].astype(F32))
        return xs, dt, Bm, Cm

    def bidir(xs, dt, Bm, Cm, init_f, init_b):
        flip = lambda t: jnp.flip(t, axis=1)
        y_f, s_f = ssd_chunked_scan(xs, dt[:, :, 0], A[0], Bm, Cm, init_f)
        y_b, s_b = ssd_chunked_scan(flip(xs), flip(dt[:, :, 1]), A[1], flip(Bm), flip(Cm), init_b)
        return y_f + flip(y_b) + d_skip * xs, s_f, s_b

    def gated_out(y, z):
        y = y.reshape(z.shape).astype(z.dtype)
        return rms_norm(y * jax.nn.silu(z), p['ssd_norm_g'])

    xc = prep(ctx)
    zero = jnp.zeros((xc[0].shape[0], SSD_GROUPS, SSD_HEADS // SSD_GROUPS, SSD_P, SSD_STATE), F32)
    y_c, s_f, s_b = bidir(*xc, zero, zero)
    y_l, _, _ = bidir(*prep(lat), s_f, s_b)
    out_lat = gated_out(y_l, lat['z'])
    out_ctx = gated_out(y_c, ctx['z']) if need_ctx else None
    return out_lat, out_ctx


def merge_branches(outs, gate_logits, w_branch, w_out):
    gates = jax.nn.sigmoid(gate_logits)
    acc = gates[..., :D_MODEL] * (outs[0] @ w_branch[0])
    for k in range(1, N_BRANCH):
        acc = acc + gates[..., k * D_MODEL:(k + 1) * D_MODEL] * (outs[k] @ w_branch[k])
    return acc @ w_out


def token_mixer(h_lat, h_ctx, p, lam_init, rope_diff, rope_mla, need_ctx):
    lat = branch_inputs(h_lat, p, rope_diff, rope_mla)
    ctx = branch_inputs(h_ctx, p, None, None)
    lv = p['diff_lambda'].astype(F32)
    lam = jnp.exp(jnp.sum(lv[0] * lv[1])) - jnp.exp(jnp.sum(lv[2] * lv[3])) + lam_init

    def diff_post(o):
        o = rms_norm(o, p['diff_subln_g']) * (1.0 - lam_init)
        return o.reshape(o.shape[:2] + (DIFF_V,))

    def mla_post(o):
        return o.reshape(o.shape[:2] + (MLA_HEADS * MLA_V,))

    dk_all = jnp.concatenate([ctx['dk'], lat['dk']], axis=1)
    dv_all = jnp.concatenate([ctx['dv'], lat['dv']], axis=1)
    mk_all = jnp.concatenate([ctx['mk'], lat['mk']], axis=1)
    mv_all = jnp.concatenate([ctx['mv'], lat['mv']], axis=1)
    diff_lat = sweep_query_blocks(lambda qb: diff_attention(qb, dk_all, dv_all, lam), lat['dq'])
    mla_lat = sweep_query_blocks(lambda qb: softmax_attention(qb, mk_all, mv_all), lat['mq'])
    ssd_lat, ssd_ctx = ssd_mixer(lat, ctx, p, need_ctx)
    y_lat = merge_branches((diff_post(diff_lat), ssd_lat, mla_post(mla_lat)), lat['gates'], p['w_branch'], p['w_out'])
    if not need_ctx:
        return y_lat, None
    diff_ctx = diff_attention(ctx['dq'], ctx['dk'], ctx['dv'], lam)
    mla_ctx = softmax_attention(ctx['mq'], ctx['mk'], ctx['mv'])
    y_ctx = merge_branches((diff_post(diff_ctx), ssd_ctx, mla_post(mla_ctx)), ctx['gates'], p['w_branch'], p['w_out'])
    return y_lat, y_ctx


def routed_expert_ffn(t, e_idx, weights, w_gate, w_up, w_down):
    T, D = t.shape
    K = e_idx.shape[1]
    E = w_gate.shape[0]
    flat_e = e_idx.reshape(-1)
    flat_tok = jnp.repeat(jnp.arange(T, dtype=jnp.int32), K)
    flat_w = weights.reshape(-1)
    order = jnp.argsort(flat_e)
    e_sorted = flat_e[order]
    counts = jnp.bincount(flat_e, length=E)
    padded = (counts + MOE_BLOCK - 1) // MOE_BLOCK * MOE_BLOCK
    pad_end = jnp.cumsum(padded)
    pad_start = pad_end - padded
    start = jnp.cumsum(counts) - counts
    slot = pad_start[e_sorted] + jnp.arange(T * K, dtype=jnp.int32) - start[e_sorted]
    n_blocks = -(-(T * K) // MOE_BLOCK) + E
    P = n_blocks * MOE_BLOCK
    tok_buf = jnp.zeros((P,), jnp.int32).at[slot].set(flat_tok[order])
    w_buf = jnp.zeros((P,), t.dtype).at[slot].set(flat_w[order])
    blk_expert = jnp.minimum(jnp.searchsorted(pad_end, jnp.arange(n_blocks, dtype=jnp.int32) * MOE_BLOCK, side='right'), E - 1)
    xb = t[tok_buf].reshape(n_blocks, MOE_BLOCK, D)

    def expert_block(args):
        xblk, e = args
        hid = jax.nn.silu(xblk @ w_gate[e]) * (xblk @ w_up[e])
        return hid @ w_down[e]

    yb = lax.map(expert_block, (xb, blk_expert)).reshape(P, D)
    return jnp.zeros_like(t).at[tok_buf].add(yb * w_buf[:, None])


def hierarchical_moe(t, group_w, group_b, expert_w, expert_b, w_gate, w_up, w_down):
    T = t.shape[0]
    p_group = jax.nn.softmax((t @ group_w).astype(F32) + group_b.astype(F32), axis=-1)
    pg, g_idx = lax.top_k(p_group, 1)
    e_logits = ((t @ expert_w).astype(F32) + expert_b.astype(F32)).reshape(T, MOE_GROUPS, MOE_PER_GROUP)
    in_group = e_logits[jnp.arange(T), g_idx[:, 0]]
    pe, e_loc = lax.top_k(jax.nn.softmax(in_group, axis=-1), MOE_TOP_K)
    weights = pg * pe / jnp.sum(pe, axis=-1, keepdims=True)
    e_idx = g_idx * MOE_PER_GROUP + e_loc
    return routed_expert_ffn(t, e_idx, weights.astype(t.dtype), w_gate, w_up, w_down)


def setup_inputs(seed: int = 0) -> dict:
    key = jax.random.key(seed)
    ks = iter(jax.random.split(key, 48))
    L = DEPTH

    def nrm(shape, scale):
        return jax.random.normal(next(ks), shape, F32) * scale

    def gain(shape):
        return 1.0 + nrm(shape, 0.02)

    dt0 = jnp.exp(jax.random.uniform(next(ks), (L, 2, SSD_HEADS), F32, math.log(1e-3), math.log(1e-1)))
    dt_bias = dt0 + jnp.log(-jnp.expm1(-dt0))
    a_log = jnp.log(jax.random.uniform(next(ks), (L, 2, SSD_HEADS), F32, 1.0, 16.0))
    return {
        'x': nrm((BATCH, SEQ, D_MODEL), 1.0),
        'c': nrm((BATCH, D_MODEL), 1.0),
        'ctx': nrm((BATCH, CTX_LEN, D_MODEL), 1.0),
        'c_ctx': nrm((D_MODEL,), 1.0),
        'ada_w': nrm((L, D_MODEL, 6 * D_MODEL), 0.5 * D_MODEL ** -0.5),
        'ada_b': nrm((L, 6 * D_MODEL), 0.02),
        'norm1_g': gain((L, D_MODEL)),
        'norm2_g': gain((L, D_MODEL)),
        'w_in': nrm((L, D_MODEL, D_IN), D_MODEL ** -0.5),
        'diff_q_g': gain((L, DIFF_DH)),
        'diff_k_g': gain((L, DIFF_DH)),
        'diff_lambda': nrm((L, 4, DIFF_DH), 0.1),
        'diff_subln_g': gain((L, DIFF_VD)),
        'ssd_conv_w': nrm((L, SSD_CONV, SSD_CONV_CH), SSD_CONV ** -0.5),
        'ssd_conv_b': nrm((L, SSD_CONV_CH), 0.02),
        'ssd_dt_bias': dt_bias,
        'ssd_A_log': a_log,
        'ssd_D': gain((L, SSD_HEADS)),
        'ssd_norm_g': gain((L, SSD_INNER)),
        'mla_cq_g': gain((L, MLA_Q_LORA)),
        'mla_ckv_g': gain((L, MLA_KV_LORA)),
        'w_uq': nrm((L, MLA_Q_LORA, MLA_HEADS * MLA_QK), MLA_Q_LORA ** -0.5),
        'w_ukv': nrm((L, MLA_KV_LORA, MLA_HEADS * (MLA_NOPE + MLA_V)), MLA_KV_LORA ** -0.5),
        'mla_q_g': gain((L, MLA_QK)),
        'mla_k_g': gain((L, MLA_QK)),
        'w_branch': nrm((L, N_BRANCH, BRANCH_W, D_MODEL), BRANCH_W ** -0.5),
        'w_out': nrm((L, D_MODEL, D_MODEL), D_MODEL ** -0.5),
        'moe_group_w': nrm((L, D_MODEL, MOE_GROUPS), D_MODEL ** -0.5),
        'moe_group_b': nrm((L, MOE_GROUPS), 0.01),
        'moe_expert_w': nrm((L, D_MODEL, MOE_EXPERTS), D_MODEL ** -0.5),
        'moe_expert_b': nrm((L, MOE_EXPERTS), 0.01),
        'moe_w_gate': nrm((L, MOE_EXPERTS, D_MODEL, MOE_HIDDEN), D_MODEL ** -0.5),
        'moe_w_up': nrm((L, MOE_EXPERTS, D_MODEL, MOE_HIDDEN), D_MODEL ** -0.5),
        'moe_w_down': nrm((L, MOE_EXPERTS, MOE_HIDDEN, D_MODEL), MOE_HIDDEN ** -0.5),
    }


def reference(x, c, ctx, c_ctx, ada_w, ada_b, norm1_g, norm2_g, w_in, diff_q_g, diff_k_g, diff_lambda,
              diff_subln_g, ssd_conv_w, ssd_conv_b, ssd_dt_bias, ssd_A_log, ssd_D, ssd_norm_g, mla_cq_g,
              mla_ckv_g, w_uq, w_ukv, mla_q_g, mla_k_g, w_branch, w_out, moe_group_w, moe_group_b,
              moe_expert_w, moe_expert_b, moe_w_gate, moe_w_up, moe_w_down):
    bsz, S, D = x.shape
    ROWS = S // GRID_W
    rope_diff = axial_rope_tables(ROWS, DIFF_DH)
    rope_mla = axial_rope_tables(ROWS, MLA_ROPE)
    x_ctx = ctx
    for l in range(DEPTH):
        last = l == DEPTH - 1
        lam_init = 0.8 - 0.6 * math.exp(-0.3 * l)
        p = {'w_in': w_in[l], 'diff_q_g': diff_q_g[l], 'diff_k_g': diff_k_g[l], 'diff_lambda': diff_lambda[l],
             'diff_subln_g': diff_subln_g[l], 'ssd_conv_w': ssd_conv_w[l], 'ssd_conv_b': ssd_conv_b[l],
             'ssd_dt_bias': ssd_dt_bias[l], 'ssd_A_log': ssd_A_log[l], 'ssd_D': ssd_D[l],
             'ssd_norm_g': ssd_norm_g[l], 'mla_cq_g': mla_cq_g[l], 'mla_ckv_g': mla_ckv_g[l],
             'w_uq': w_uq[l], 'w_ukv': w_ukv[l], 'mla_q_g': mla_q_g[l], 'mla_k_g': mla_k_g[l],
             'w_branch': w_branch[l], 'w_out': w_out[l]}
        mod_lat = (jax.nn.silu(c) @ ada_w[l] + ada_b[l])[:, None, :]
        mod_ctx = jax.nn.silu(c_ctx) @ ada_w[l] + ada_b[l]
        sh1, sc1, g1, sh2, sc2, g2 = jnp.split(mod_lat, 6, axis=-1)
        csh1, csc1, cg1, csh2, csc2, cg2 = jnp.split(mod_ctx, 6, axis=-1)

        h_lat = modulate(rms_norm(x, norm1_g[l]), sh1, sc1)
        h_ctx = modulate(rms_norm(x_ctx, norm1_g[l]), csh1, csc1)
        y_lat, y_ctx = token_mixer(h_lat, h_ctx, p, lam_init, rope_diff, rope_mla, not last)
        x = x + g1 * y_lat
        h2 = modulate(rms_norm(x, norm2_g[l]), sh2, sc2).reshape(-1, D)
        moe_args = (moe_group_w[l], moe_group_b[l], moe_expert_w[l], moe_expert_b[l],
                    moe_w_gate[l], moe_w_up[l], moe_w_down[l])
        if last:
            x = x + g2 * hierarchical_moe(h2, *moe_args).reshape(bsz, S, D)
        else:
            x_ctx = x_ctx + cg1 * y_ctx
            h2c = modulate(rms_norm(x_ctx, norm2_g[l]), csh2, csc2).reshape(-1, D)
            f = hierarchical_moe(jnp.concatenate([h2, h2c], axis=0), *moe_args)
            x = x + g2 * f[:bsz * S].reshape(bsz, S, D)
            x_ctx = x_ctx + cg2 * f[bsz * S:].reshape(x_ctx.shape)
    return x
```

```python
import functools
import math

import jax
import jax.numpy as jnp
import numpy as np
from jax import lax
from jax.experimental import pallas as pl
from jax.experimental.pallas import tpu as pltpu

F32 = jnp.float32
BF16 = jnp.bfloat16
I32 = jnp.int32

D_MODEL = 1024
GRID_W = 64
ROPE_BASE = 10000.0
EPS = 1e-6
DIFF_HEADS = 4
DIFF_DH = 64
DIFF_VD = 2 * DIFF_DH
SSD_HEADS = 8
SSD_P = 64
SSD_INNER = SSD_HEADS * SSD_P
SSD_GROUPS = 2
SSD_STATE = 128
SSD_CONV = 5
SSD_CHUNK = 128
MLA_HEADS = 8
MLA_NOPE = 64
MLA_ROPE = 32
MLA_V = 64
MLA_Q_LORA = 384
MLA_KV_LORA = 256
MLA_QK = MLA_NOPE + MLA_ROPE
N_BRANCH = 3
MOE_GROUPS = 4
MOE_PER_GROUP = 8
MOE_EXPERTS = MOE_GROUPS * MOE_PER_GROUP
MOE_TOP_K = 2
MOE_HIDDEN = 256

LANES = 128
SUBLANES = 8
TM = 256
MOE_BM = 256
VMEM_LIMIT = 56 * 1024 * 1024
HALO = SUBLANES
MOD_ROWS = 16
NEG = -1e30

SEG_DQ = (0, 512)
SEG_DK = (512, 1024)
SEG_DV = (1024, 1536)
SEG_Z = (1536, 2048)
SEG_XBC = (2048, 3072)
SEG_DT = (3072, 3200)
SEG_CQ = (3200, 3584)
SEG_CKV = (3584, 3840)
SEG_KR = (3840, 3968)
SEG_GATES = (3968, 7040)
W_IN_PAD = 7040
MLA_HP = 128

R_E1, R_E2, R_R1, R_R2, R_W1, R_W2 = 0, 1, 2, 3, 4, 5


def _cparams(sem, vmem=VMEM_LIMIT):
    return pltpu.CompilerParams(dimension_semantics=sem, vmem_limit_bytes=vmem)


def _dot(a, b):
    return jnp.dot(a, b, preferred_element_type=F32)


def _dot_nt(a, b):
    return lax.dot_general(a, b, (((1,), (1,)), ((), ())), preferred_element_type=F32)


def _split2(x):
    hi = x.astype(BF16)
    lo = (x - hi.astype(F32)).astype(BF16)
    return hi, lo


def _split3(x):
    hi = x.astype(BF16)
    r = x - hi.astype(F32)
    mid = r.astype(BF16)
    lo = (r - mid.astype(F32)).astype(BF16)
    return hi, mid, lo


def _silu(x):
    return x * jax.nn.sigmoid(x)


def _rms(x):
    return x * lax.rsqrt(jnp.mean(x * x, axis=-1, keepdims=True) + EPS)


def _group_rms_scale(x, gmat, emat):
    ms = _dot((x * x).astype(BF16), gmat)
    r = lax.rsqrt(ms + EPS)
    r_hi, r_lo = _split2(r)
    return x * (_dot(r_hi, emat) + _dot(r_lo, emat))


def _rope(x, cos, sin_a, sin_b, quarter):
    w = x.shape[-1]
    return x * cos + pltpu.roll(x, quarter, 1) * sin_a + pltpu.roll(x, w - quarter, 1) * sin_b


def _const_spec(shape):
    nd = len(shape)
    return pl.BlockSpec(shape, lambda *_: (0,) * nd, pipeline_mode=pl.Buffered(1))


def _mod_kernel(c_ref, w_ref, b_ref, o_ref):
    s = _silu(c_ref[...])
    o_ref[0] = jnp.dot(s, w_ref[0], precision=lax.Precision.HIGHEST, preferred_element_type=F32) + b_ref[0]


def _modulation(cc, ada_w, ada_b):
    depth, d, n = ada_w.shape
    tn = 1536
    return pl.pallas_call(
        _mod_kernel,
        grid=(depth, n // tn),
        in_specs=[pl.BlockSpec((MOD_ROWS, d), lambda l, j: (0, 0)),
                  pl.BlockSpec((1, d, tn), lambda l, j: (l, 0, j)),
                  pl.BlockSpec((1, 1, tn), lambda l, j: (l, 0, j))],
        out_specs=pl.BlockSpec((1, MOD_ROWS, tn), lambda l, j: (l, 0, j)),
        out_shape=jax.ShapeDtypeStruct((depth, MOD_ROWS, n), F32),
        compiler_params=_cparams(("arbitrary", "arbitrary")),
        name="adaln_mod",
    )(cc, ada_w, ada_b.reshape(depth, 1, n))


def _inproj_kernel(x_ref, mod_ref, g1_ref, w_ref, wuq_ref, wuk_ref, wuv_ref,
                   gd_ref, ed_ref, gq_ref, eq_ref, gk_ref, ek_ref, ekr_ref,
                   cosd_ref, sad_ref, sbd_ref, cosq_ref, saq_ref, sbq_ref, cosk_ref, sak_ref, sbk_ref,
                   dqg_ref, dkg_ref, cqg_ref, ckvg_ref, mqg_ref, mkg_ref, krg_ref,
                   dq_ref, dk_ref, dv_ref, z_ref, xbc_ref, dt_ref, mq_ref, mk_ref, mv_ref, gates_ref,
                   *, n_batch):
    t = pl.program_id(0)
    b = pl.program_id(1)
    row = jnp.where(t == 0, n_batch, b)
    shift = mod_ref[pl.ds(row, 1), 0:D_MODEL]
    scale = mod_ref[pl.ds(row, 1), D_MODEL:2 * D_MODEL]
    xn = _rms(x_ref[0]) * g1_ref[...]
    h = (xn * (1.0 + scale) + shift).astype(BF16)

    def proj(seg):
        return _dot(h, w_ref[:, seg[0]:seg[1]])

    cosd, sad, sbd = cosd_ref[...], sad_ref[...], sbd_ref[...]
    dq = _group_rms_scale(proj(SEG_DQ), gd_ref[...], ed_ref[...]) * dqg_ref[...]
    dq_ref[0] = _rope(dq, cosd, sad, sbd, DIFF_DH // 4).astype(BF16)
    dk = _group_rms_scale(proj(SEG_DK), gd_ref[...], ed_ref[...]) * dkg_ref[...]
    dk_ref[0] = _rope(dk, cosd, sad, sbd, DIFF_DH // 4).astype(BF16)
    dv_ref[0] = proj(SEG_DV).astype(BF16)
    z_ref[0] = proj(SEG_Z).astype(BF16)
    xbc_ref[0] = proj(SEG_XBC)
    dt_ref[0] = proj(SEG_DT)
    gates_ref[0] = jax.nn.sigmoid(proj(SEG_GATES)).astype(BF16)

    cq = _rms(proj(SEG_CQ)) * cqg_ref[...]
    q = _dot(cq.astype(BF16), wuq_ref[...])
    q = _group_rms_scale(q, gq_ref[...], eq_ref[...]) * mqg_ref[...]
    mq_ref[0] = _rope(q, cosq_ref[...], saq_ref[...], sbq_ref[...], MLA_ROPE // 4).astype(BF16)

    ckv = (_rms(proj(SEG_CKV)) * ckvg_ref[...]).astype(BF16)
    kn = _dot(ckv, wuk_ref[...])
    kn = _group_rms_scale(kn, gk_ref[...], ek_ref[...]) * mkg_ref[...]
    mv_ref[0] = _dot(ckv, wuv_ref[...]).astype(BF16)
    kr = proj(SEG_KR)
    kr = kr * lax.rsqrt(jnp.sum(kr * kr, axis=-1, keepdims=True) * (1.0 / MLA_ROPE) + EPS) * krg_ref[...]
    kr = _rope(kr, cosk_ref[...], sak_ref[...], sbk_ref[...], MLA_ROPE // 4).astype(BF16)
    mk_ref[0] = (kn + _dot(kr, ekr_ref[...])).astype(BF16)


def _rope_tables(n_ctx, n_lat, dim, lane_layout):
    quarter = dim // 4
    inv_freq = ROPE_BASE ** (-jnp.arange(quarter, dtype=F32) / quarter)
    s = jnp.arange(n_lat, dtype=I32)
    ar = (s // GRID_W).astype(F32)[:, None] * inv_freq
    ac = (s % GRID_W).astype(F32)[:, None] * inv_freq
    ang = jnp.concatenate([ar, ar, ac, ac], axis=-1)
    ang = jnp.concatenate([jnp.zeros((n_ctx, dim), F32), ang], axis=0)
    odd = jnp.asarray((np.arange(dim) // quarter) % 2 == 1)
    cos_d, sin_d = jnp.cos(ang), jnp.sin(ang)
    units = (cos_d, jnp.where(odd, sin_d, 0.0), jnp.where(odd, 0.0, -sin_d))
    out = []
    for unit, fill in zip(units, (1.0, 0.0, 0.0)):
        pieces = [unit if item == 'rope' else jnp.full((n_ctx + n_lat, item), fill, F32) for item in lane_layout]
        out.append(jnp.concatenate(pieces, axis=-1))
    return out


def _group_mats(width, groups):
    g = np.zeros((width, LANES), np.float32)
    e = np.zeros((LANES, width), np.float32)
    for i, (st, sz) in enumerate(groups):
        g[st:st + sz, i] = 1.0 / sz
        e[i, st:st + sz] = 1.0
    return jnp.asarray(g, BF16), jnp.asarray(e, BF16)


def _static_tables(n_ctx, n_lat):
    tabs = {}
    tabs['gd'], tabs['ed'] = _group_mats(512, [(i * DIFF_DH, DIFF_DH) for i in range(2 * DIFF_HEADS)])
    qgroups = []
    for h in range(MLA_HEADS):
        qgroups += [(h * MLA_HP, MLA_NOPE), (h * MLA_HP + MLA_NOPE, MLA_ROPE)]
    tabs['gq'], tabs['eq'] = _group_mats(MLA_HEADS * MLA_HP, qgroups)
    tabs['gk'], tabs['ek'] = _group_mats(MLA_HEADS * MLA_HP, [(h * MLA_HP, MLA_NOPE) for h in range(MLA_HEADS)])
    ekr = np.zeros((LANES, MLA_HEADS * MLA_HP), np.float32)
    for h in range(MLA_HEADS):
        for j in range(MLA_ROPE):
            ekr[j, h * MLA_HP + MLA_NOPE + j] = 1.0
    tabs['ekr'] = jnp.asarray(ekr, BF16)
    tabs['cosd'], tabs['sad'], tabs['sbd'] = _rope_tables(n_ctx, n_lat, DIFF_DH, ['rope'] * (2 * DIFF_HEADS))
    tabs['cosq'], tabs['saq'], tabs['sbq'] = _rope_tables(
        n_ctx, n_lat, MLA_ROPE, [MLA_NOPE, 'rope', MLA_HP - MLA_QK] * MLA_HEADS)
    tabs['cosk'], tabs['sak'], tabs['sbk'] = _rope_tables(n_ctx, n_lat, MLA_ROPE, ['rope', LANES - MLA_ROPE])
    return tabs


def _in_projection(xs, mod_l, p, tabs):
    n_batch, n_rows, d = xs.shape
    nt = n_rows // TM
    row_spec = lambda w: pl.BlockSpec((1, TM, w), lambda t, b: (b, t, 0))
    tab_spec = lambda w: pl.BlockSpec((TM, w), lambda t, b: (t, 0))
    consts = [mod_l, p['g1'], p['w_in'], p['w_uq'], p['w_uk'], p['w_uv'],
              tabs['gd'], tabs['ed'], tabs['gq'], tabs['eq'], tabs['gk'], tabs['ek'], tabs['ekr']]
    rope = [tabs['cosd'], tabs['sad'], tabs['sbd'], tabs['cosq'], tabs['saq'], tabs['sbq'],
            tabs['cosk'], tabs['sak'], tabs['sbk']]
    gains = [p['dq_g'], p['dk_g'], p['cq_g'], p['ckv_g'], p['mq_g'], p['mk_g'], p['kr_g']]
    out_w = [(512, BF16), (512, BF16), (512, BF16), (512, BF16), (1024, F32), (LANES, F32),
             (MLA_HEADS * MLA_HP, BF16), (MLA_HEADS * MLA_HP, BF16), (512, BF16), (N_BRANCH * D_MODEL, BF16)]
    return pl.pallas_call(
        functools.partial(_inproj_kernel, n_batch=n_batch),
        grid=(nt, n_batch),
        in_specs=([row_spec(d)] + [_const_spec(a.shape) for a in consts]
                  + [tab_spec(a.shape[1]) for a in rope] + [_const_spec(a.shape) for a in gains]),
        out_specs=[row_spec(w) for w, _ in out_w],
        out_shape=[jax.ShapeDtypeStruct((n_batch, n_rows, w), dt) for w, dt in out_w],
        compiler_params=_cparams(("arbitrary", "arbitrary")),
        name="in_proj",
    )(xs, *consts, *rope, *gains)


def _ssd_chunk_of_step(s, direction, n_ctx_chunks, n_chunks):
    if direction == 0:
        return s
    return jnp.where(s < n_ctx_chunks, n_ctx_chunks - 1 - s, n_chunks - 1 + n_ctx_chunks - s)


def _ssd_kernel(*refs, direction, n_ctx_chunks, n_chunks):
    if direction == 0:
        (xbc_ref, prev_ref, next_ref, dt_ref, cw_ref, cb_ref, dtb_ref, alog_ref, y_ref, state_ref) = refs
    else:
        (xbc_ref, prev_ref, next_ref, dt_ref, cw_ref, cb_ref, dtb_ref, alog_ref,
         yf_ref, z_ref, dskip_ref, ng_ref, y_ref, state_ref) = refs
    s = pl.program_id(1)
    c = _ssd_chunk_of_step(s, direction, n_ctx_chunks, n_chunks)
    ck = SSD_CHUNK

    @pl.when(s == 0)
    def _():
        state_ref[...] = jnp.zeros_like(state_ref)

    first = jnp.logical_or(c == 0, c == n_ctx_chunks)
    last = jnp.logical_or(c == n_ctx_chunks - 1, c == n_chunks - 1)
    prev = jnp.where(first, 0.0, prev_ref[0])
    nxt = jnp.where(last, 0.0, next_ref[0])
    full = jnp.concatenate([prev, xbc_ref[0], nxt], axis=0)
    n_full = ck + 2 * HALO
    half = (SSD_CONV - 1) // 2
    conv = None
    for k in range(SSD_CONV):
        sh = (half - k) % n_full
        rolled = full if sh == 0 else pltpu.roll(full, sh, 0)
        term = rolled[HALO:HALO + ck] * cw_ref[k:k + 1, :]
        conv = term if conv is None else conv + term
    u = _silu(conv + cb_ref[...])
    xs = u[:, :SSD_INNER]
    bm = u[:, SSD_INNER:SSD_INNER + SSD_GROUPS * SSD_STATE]
    cm = u[:, SSD_INNER + SSD_GROUPS * SSD_STATE:]

    dt = jax.nn.softplus(dt_ref[0] + dtb_ref[...])
    a = dt * (-jnp.exp(alog_ref[...]))
    ri = lax.broadcasted_iota(I32, (ck, ck), 0)
    ci = lax.broadcasted_iota(I32, (ck, ck), 1)
    valid = (ci <= ri) if direction == 0 else (ci >= ri)
    tri = jnp.where(valid, 1.0, 0.0).astype(BF16)
    a1, a2, a3 = _split3(a)
    acum = _dot(tri, a1) + _dot(tri, a2) + _dot(tri, a3)
    end_row = ck - 1 if direction == 0 else 0
    atot = acum[end_row:end_row + 1, :]
    w_end = jnp.exp(atot - acum) * dt
    e_in = jnp.exp(acum)
    cdec = jnp.exp(atot)
    acum_t = acum.T
    dt_t = dt.T
    lane = lax.broadcasted_iota(I32, (1, LANES), 1)
    left = lane < SSD_P

    pairs_per_group = SSD_HEADS // SSD_GROUPS // 2
    ys = []
    for g in range(SSD_GROUPS):
        bg = bm[:, g * SSD_STATE:(g + 1) * SSD_STATE]
        cg = cm[:, g * SSD_STATE:(g + 1) * SSD_STATE].astype(BF16)
        cb = _dot_nt(cg, bg.astype(BF16))
        bg_t = bg.T.astype(BF16)
        for pp in range(pairs_per_group):
            pr = g * pairs_per_group + pp
            xs_p = xs[:, pr * LANES:(pr + 1) * LANES]
            c0 = direction * SSD_HEADS + 2 * pr
            yd = None
            for side in range(2):
                col = c0 + side
                seg = acum[:, col:col + 1] - acum_t[col:col + 1, :]
                decay = jnp.where(valid, jnp.exp(jnp.where(valid, seg, 0.0)), 0.0)
                mix = (cb * decay * dt_t[col:col + 1, :]).astype(BF16)
                keep = left if side == 0 else jnp.logical_not(left)
                part = _dot(mix, jnp.where(keep, xs_p, 0.0).astype(BF16))
                yd = part if yd is None else yd + part
            pick = lambda m: jnp.where(left, m[:, c0:c0 + 1], m[:, c0 + 1:c0 + 2])
            h_in = state_ref[pr]
            y_off = _dot(cg, h_in.astype(BF16)) * pick(e_in)
            xw = (xs_p * pick(w_end)).astype(BF16)
            state_ref[pr] = h_in * pick(cdec) + _dot(bg_t, xw)
            ys.append(yd + y_off)
    y = jnp.concatenate(ys, axis=-1)
    if direction == 0:
        y_ref[0] = y
    else:
        y = y + yf_ref[0] + dskip_ref[...] * xs
        y = y * _silu(z_ref[0].astype(F32))
        y_ref[0] = (_rms(y) * ng_ref[...]).astype(BF16)


def _ssd_scan(xbc, dt, p, n_ctx, direction, yf=None, z=None):
    n_batch, n_rows, wc = xbc.shape
    ck = SSD_CHUNK
    n_chunks = n_rows // ck
    n_ctx_chunks = n_ctx // ck
    n_halo_blocks = n_rows // HALO
    per = ck // HALO
    cmap = lambda s: _ssd_chunk_of_step(s, direction, n_ctx_chunks, n_chunks)
    cur = lambda w: pl.BlockSpec((1, ck, w), lambda b, s: (b, cmap(s), 0))
    in_specs = [cur(wc),
                pl.BlockSpec((1, HALO, wc), lambda b, s: (b, jnp.maximum(cmap(s) * per - 1, 0), 0)),
                pl.BlockSpec((1, HALO, wc), lambda b, s: (b, jnp.minimum((cmap(s) + 1) * per, n_halo_blocks - 1), 0)),
                cur(LANES),
                _const_spec(p['conv_w'].shape), _const_spec(p['conv_b'].shape),
                _const_spec(p['dt_bias'].shape), _const_spec(p['a_log'].shape)]
    args = [xbc, xbc, xbc, dt, p['conv_w'], p['conv_b'], p['dt_bias'], p['a_log']]
    if direction == 0:
        out_dtype = F32
    else:
        in_specs += [cur(SSD_INNER), cur(SSD_INNER), _const_spec(p['d_skip'].shape), _const_spec(p['ssd_g'].shape)]
        args += [yf, z, p['d_skip'], p['ssd_g']]
        out_dtype = BF16
    return pl.pallas_call(
        functools.partial(_ssd_kernel, direction=direction, n_ctx_chunks=n_ctx_chunks, n_chunks=n_chunks),
        grid=(n_batch, n_chunks),
        in_specs=in_specs,
        out_specs=cur(SSD_INNER),
        out_shape=jax.ShapeDtypeStruct((n_batch, n_rows, SSD_INNER), out_dtype),
        scratch_shapes=[pltpu.VMEM((SSD_HEADS // 2, SSD_STATE, LANES), F32)],
        compiler_params=_cparams(("arbitrary", "arbitrary")),
        name="ssd_fwd" if direction == 0 else "ssd_bwd",
    )(*args)


def _softmax_parts(s):
    m = jnp.max(s, axis=-1, keepdims=True)
    e = jnp.exp(s - m)
    return e, 1.0 / jnp.sum(e, axis=-1, keepdims=True)


def _diff_attn_kernel(q_ref, k_ref, v_ref, lam_ref, sg_ref, o_ref, *, n_ctx, lam_init, t0):
    lv = lam_ref[...]
    lam = (jnp.exp(jnp.sum(lv[0:1] * lv[1:2], axis=-1, keepdims=True))
           - jnp.exp(jnp.sum(lv[2:3] * lv[3:4], axis=-1, keepdims=True)) + lam_init)
    lane = lax.broadcasted_iota(I32, (1, DIFF_VD), 1)
    q = q_ref[0]
    q0 = jnp.where(lane < DIFF_DH, q, jnp.zeros_like(q))
    q1 = jnp.where(lane < DIFF_DH, jnp.zeros_like(q), q)

    def attend(n_keys):
        k = k_ref[0, 0:n_keys, :]
        e0, r0 = _softmax_parts(_dot_nt(q0, k))
        e1, r1 = _softmax_parts(_dot_nt(q1, k))
        w = (e0 * r0 - e1 * (lam * r1)).astype(BF16)
        o = _dot(w, v_ref[0, 0:n_keys, :])
        o_ref[0] = (_rms(o) * sg_ref[...]).astype(BF16)

    is_ctx = t0 + pl.program_id(2) == 0

    @pl.when(is_ctx)
    def _():
        attend(n_ctx)

    @pl.when(jnp.logical_not(is_ctx))
    def _():
        attend(k_ref.shape[1])


def _diff_attention(dq, dk, dv, lam_p, subln_g, n_ctx, lam_init, t0):
    n_batch, n_rows, _ = dq.shape
    nq = n_rows // TM - t0
    return pl.pallas_call(
        functools.partial(_diff_attn_kernel, n_ctx=n_ctx, lam_init=lam_init, t0=t0),
        grid=(n_batch, DIFF_HEADS, nq),
        in_specs=[pl.BlockSpec((1, TM, DIFF_VD), lambda b, h, t: (b, t0 + t, h)),
                  pl.BlockSpec((1, n_rows, DIFF_VD), lambda b, h, t: (b, 0, h)),
                  pl.BlockSpec((1, n_rows, DIFF_VD), lambda b, h, t: (b, 0, h)),
                  _const_spec(lam_p.shape), _const_spec(subln_g.shape)],
        out_specs=pl.BlockSpec((1, TM, DIFF_VD), lambda b, h, t: (b, t, h)),
        out_shape=jax.ShapeDtypeStruct((n_batch, nq * TM, DIFF_HEADS * DIFF_VD), BF16),
        compiler_params=_cparams(("arbitrary", "arbitrary", "arbitrary")),
        name="diff_attn",
    )(dq, dk, dv, lam_p, subln_g)


def _mla_attn_kernel(q_ref, k_ref, v_ref, o_ref, *, n_ctx, t0):
    lane = lax.broadcasted_iota(I32, (1, 2 * MLA_V), 1)

    def attend(n_keys):
        v = v_ref[0, 0:n_keys, :]
        o = None
        for side in range(2):
            sl = slice(side * MLA_HP, (side + 1) * MLA_HP)
            e, r = _softmax_parts(_dot_nt(q_ref[0, :, sl], k_ref[0, 0:n_keys, sl]))
            keep = (lane < MLA_V) if side == 0 else (lane >= MLA_V)
            part = _dot(e.astype(BF16), jnp.where(keep, v, jnp.zeros_like(v))) * r
            o = part if o is None else o + part
        o_ref[0] = o.astype(BF16)

    is_ctx = t0 + pl.program_id(2) == 0

    @pl.when(is_ctx)
    def _():
        attend(n_ctx)

    @pl.when(jnp.logical_not(is_ctx))
    def _():
        attend(k_ref.shape[1])


def _mla_attention(mq, mk, mv, n_ctx, t0):
    n_batch, n_rows, _ = mq.shape
    nq = n_rows // TM - t0
    return pl.pallas_call(
        functools.partial(_mla_attn_kernel, n_ctx=n_ctx, t0=t0),
        grid=(n_batch, MLA_HEADS // 2, nq),
        in_specs=[pl.BlockSpec((1, TM, 2 * MLA_HP), lambda b, h, t: (b, t0 + t, h)),
                  pl.BlockSpec((1, n_rows, 2 * MLA_HP), lambda b, h, t: (b, 0, h)),
                  pl.BlockSpec((1, n_rows, 2 * MLA_V), lambda b, h, t: (b, 0, h))],
        out_specs=pl.BlockSpec((1, TM, 2 * MLA_V), lambda b, h, t: (b, t, h)),
        out_shape=jax.ShapeDtypeStruct((n_batch, nq * TM, MLA_HEADS * MLA_V), BF16),
        compiler_params=_cparams(("arbitrary", "arbitrary", "arbitrary")),
        name="mla_attn",
    )(mq, mk, mv)


def _merge_kernel(x_ref, da_ref, ss_ref, ma_ref, gates_ref, mod_ref, wb_ref, wo_ref, g2_ref,
                  rwh_ref, rwl_ref, rb_ref,
                  xo_ref, h2_ref, route_ref, cnt_ref, base_ref, *, n_batch, t0):
    b = pl.program_id(0)
    t = pl.program_id(1)

    @pl.when(jnp.logical_and(b == 0, t == 0))
    def _():
        base_ref[...] = jnp.zeros_like(base_ref)

    row = jnp.where(t0 + t == 0, n_batch, b)
    gate1 = mod_ref[pl.ds(row, 1), 2 * D_MODEL:3 * D_MODEL]
    shift2 = mod_ref[pl.ds(row, 1), 3 * D_MODEL:4 * D_MODEL]
    scale2 = mod_ref[pl.ds(row, 1), 4 * D_MODEL:5 * D_MODEL]

    acc = None
    for k, o_ref in enumerate((da_ref, ss_ref, ma_ref)):
        term = gates_ref[0, :, k * D_MODEL:(k + 1) * D_MODEL].astype(F32) * _dot(o_ref[0], wb_ref[k])
        acc = term if acc is None else acc + term
    x = x_ref[0] + gate1 * _dot(acc.astype(BF16), wo_ref[...])
    xo_ref[0] = x
    h2 = _rms(x) * g2_ref[...] * (1.0 + scale2) + shift2
    h2_ref[0] = h2

    h_hi, h_lo = _split2(h2)
    lg = _dot(h_hi, rwh_ref[...]) + _dot(h_lo, rwh_ref[...]) + _dot(h_hi, rwl_ref[...]) + rb_ref[...]
    lane = lax.broadcasted_iota(I32, lg.shape, 1)
    lane_f = lane.astype(F32)
    big = float(LANES)
    is_g = lane < MOE_GROUPS
    gl = jnp.where(is_g, lg, NEG)
    gmax = jnp.max(gl, axis=-1, keepdims=True)
    gidx = jnp.min(jnp.where(gl == gmax, lane_f, big), axis=-1, keepdims=True)
    pg = 1.0 / jnp.sum(jnp.where(is_g, jnp.exp(gl - gmax), 0.0), axis=-1, keepdims=True)
    lo_lane = MOE_GROUPS + gidx * MOE_PER_GROUP
    in_grp = jnp.logical_and(lane_f >= lo_lane, lane_f < lo_lane + MOE_PER_GROUP)
    el = jnp.where(in_grp, lg, NEG)
    m1 = jnp.max(el, axis=-1, keepdims=True)
    i1 = jnp.min(jnp.where(el == m1, lane_f, big), axis=-1, keepdims=True)
    el2 = jnp.where(lane_f == i1, NEG, el)
    m2 = jnp.max(el2, axis=-1, keepdims=True)
    i2 = jnp.min(jnp.where(el2 == m2, lane_f, big), axis=-1, keepdims=True)
    tt = jnp.exp(m2 - m1)
    w1 = pg / (1.0 + tt)
    w2 = pg * tt / (1.0 + tt)
    e1 = i1 - MOE_GROUPS
    e2 = i2 - MOE_GROUPS

    oh1 = lane_f == e1
    oh2 = lane_f == e2
    oh = jnp.where(jnp.logical_or(oh1, oh2), 1.0, 0.0)
    tm = oh.shape[0]
    ri = lax.broadcasted_iota(I32, (tm, tm), 0)
    ci = lax.broadcasted_iota(I32, (tm, tm), 1)
    strict = jnp.where(ci < ri, 1.0, 0.0).astype(BF16)
    base = base_ref[0:1, :]
    rank_all = _dot(strict, oh.astype(BF16)) + base
    r1 = jnp.sum(jnp.where(oh1, rank_all, 0.0), axis=-1, keepdims=True)
    r2 = jnp.sum(jnp.where(oh2, rank_all, 0.0), axis=-1, keepdims=True)
    new_base = base + jnp.sum(oh, axis=0, keepdims=True)
    base_ref[...] = jnp.broadcast_to(new_base, base_ref.shape)
    cnt_ref[...] = jnp.broadcast_to(new_base, cnt_ref.shape)

    rec = jnp.zeros(lg.shape, F32)
    for ln, val in ((R_E1, e1), (R_E2, e2), (R_R1, r1), (R_R2, r2), (R_W1, w1), (R_W2, w2)):
        rec = jnp.where(lane == ln, val, rec)
    route_ref[0] = rec


def _merge(xs, da, ss, ma, gates, mod_l, p, t0):
    n_batch, n_rows, d = xs.shape
    nt = n_rows // TM - t0
    rows_out = nt * TM
    in_row = lambda w: pl.BlockSpec((1, TM, w), lambda b, t: (b, t0 + t, 0))
    out_row = lambda w: pl.BlockSpec((1, TM, w), lambda b, t: (b, t, 0))
    consts = [mod_l, p['w_branch'], p['w_out'], p['g2'], p['rw_hi'], p['rw_lo'], p['rb']]
    return pl.pallas_call(
        functools.partial(_merge_kernel, n_batch=n_batch, t0=t0),
        grid=(n_batch, nt),
        in_specs=[in_row(d), out_row(512), in_row(512), out_row(512), in_row(N_BRANCH * D_MODEL)]
                 + [_const_spec(a.shape) for a in consts],
        out_specs=[out_row(d), out_row(d), out_row(LANES), pl.BlockSpec((SUBLANES, LANES), lambda b, t: (0, 0))],
        out_shape=[jax.ShapeDtypeStruct((n_batch, rows_out, d), F32),
                   jax.ShapeDtypeStruct((n_batch, rows_out, d), F32),
                   jax.ShapeDtypeStruct((n_batch, rows_out, LANES), F32),
                   jax.ShapeDtypeStruct((SUBLANES, LANES), F32)],
        scratch_shapes=[pltpu.VMEM((SUBLANES, LANES), F32)],
        compiler_params=_cparams(("arbitrary", "arbitrary")),
        name="merge_route",
    )(xs, da, ss, ma, gates, *consts)


def _row_copy(src_ref, src_row, dst_ref, dst_row, sem):
    return pltpu.make_async_copy(src_ref.at[pl.ds(src_row, 1), :], dst_ref.at[pl.ds(dst_row, 1), :], sem)


def _dispatch_kernel(slots_ref, h2_ref, xs_in_ref, xs_ref, sem):
    del xs_in_ref
    tm = h2_ref.shape[1]
    src = h2_ref.at[0]

    def issue(r, carry):
        _row_copy(src, r, xs_ref, slots_ref[0, 0, r], sem).start()
        _row_copy(src, r, xs_ref, slots_ref[0, 0, tm + r], sem).start()
        return carry

    lax.fori_loop(0, tm, issue, 0)

    def drain(r, carry):
        _row_copy(src, 0, xs_ref, 0, sem).wait()
        return carry

    lax.fori_loop(0, MOE_TOP_K * tm, drain, 0)


def _dispatch(h2, slots, n_slots):
    n_batch, n_rows, d = h2.shape
    nt = n_rows // TM
    zeros = jnp.zeros((n_slots, d), F32)
    return pl.pallas_call(
        _dispatch_kernel,
        grid=(n_batch, nt),
        in_specs=[pl.BlockSpec((1, 1, MOE_TOP_K * TM), lambda b, t: (b * nt + t, 0, 0), memory_space=pltpu.SMEM),
                  pl.BlockSpec((1, TM, d), lambda b, t: (b, t, 0)),
                  pl.BlockSpec(memory_space=pl.ANY)],
        out_specs=pl.BlockSpec(memory_space=pl.ANY),
        out_shape=jax.ShapeDtypeStruct((n_slots, d), F32),
        scratch_shapes=[pltpu.SemaphoreType.DMA(())],
        input_output_aliases={2: 0},
        compiler_params=_cparams(("arbitrary", "arbitrary")),
        name="moe_dispatch",
    )(slots, h2, zeros)


def _expert_kernel(be_ref, nu_ref, xs_ref, wgu_ref, wd_ref, y_ref):
    i = pl.program_id(0)

    @pl.when(i < nu_ref[0])
    def _():
        gu = _dot(xs_ref[...].astype(BF16), wgu_ref[0])
        hid = _silu(gu[:, :MOE_HIDDEN]) * gu[:, MOE_HIDDEN:]
        y_ref[...] = _dot(hid.astype(BF16), wd_ref[0])

    @pl.when(i >= nu_ref[0])
    def _():
        y_ref[...] = jnp.zeros_like(y_ref)


def _expert_ffn(xs, blk_expert, n_used, w_gu, w_d):
    n_slots, d = xs.shape
    n_blocks = n_slots // MOE_BM
    grid_spec = pltpu.PrefetchScalarGridSpec(
        num_scalar_prefetch=2,
        grid=(n_blocks,),
        in_specs=[pl.BlockSpec((MOE_BM, d), lambda i, be, nu: (jnp.minimum(i, nu[0] - 1), 0)),
                  pl.BlockSpec((1, d, 2 * MOE_HIDDEN), lambda i, be, nu: (be[i], 0, 0)),
                  pl.BlockSpec((1, MOE_HIDDEN, d), lambda i, be, nu: (be[i], 0, 0))],
        out_specs=pl.BlockSpec((MOE_BM, d), lambda i, be, nu: (i, 0)),
    )
    return pl.pallas_call(
        _expert_kernel,
        grid_spec=grid_spec,
        out_shape=jax.ShapeDtypeStruct((n_slots, d), F32),
        compiler_params=_cparams(("arbitrary",)),
        name="moe_experts",
    )(blk_expert, n_used, xs, w_gu, w_d)


def _combine_kernel(slots_ref, x_ref, route_ref, mod_ref, y_ref, o_ref, buf_ref, sem, *, n_batch, t0):
    b = pl.program_id(0)
    t = pl.program_id(1)
    tm = x_ref.shape[1]

    def issue(r, carry):
        _row_copy(y_ref, slots_ref[0, 0, r], buf_ref.at[0], r, sem).start()
        _row_copy(y_ref, slots_ref[0, 0, tm + r], buf_ref.at[1], r, sem).start()
        return carry

    lax.fori_loop(0, tm, issue, 0)

    def drain(r, carry):
        _row_copy(y_ref, 0, buf_ref.at[0], 0, sem).wait()
        return carry

    lax.fori_loop(0, MOE_TOP_K * tm, drain, 0)

    row = jnp.where(t0 + t == 0, n_batch, b)
    gate2 = mod_ref[pl.ds(row, 1), 5 * D_MODEL:6 * D_MODEL]
    rec = route_ref[0]
    f = rec[:, R_W1:R_W1 + 1] * buf_ref[0] + rec[:, R_W2:R_W2 + 1] * buf_ref[1]
    o_ref[0] = x_ref[0] + gate2 * f


def _combine(x_mid, route, slots, y, mod_l, t0):
    n_batch, n_rows, d = x_mid.shape
    nt = n_rows // TM
    row = lambda w: pl.BlockSpec((1, TM, w), lambda b, t: (b, t, 0))
    return pl.pallas_call(
        functools.partial(_combine_kernel, n_batch=n_batch, t0=t0),
        grid=(n_batch, nt),
        in_specs=[pl.BlockSpec((1, 1, MOE_TOP_K * TM), lambda b, t: (b * nt + t, 0, 0), memory_space=pltpu.SMEM),
                  row(d), row(LANES), _const_spec(mod_l.shape),
                  pl.BlockSpec(memory_space=pl.ANY)],
        out_specs=row(d),
        out_shape=jax.ShapeDtypeStruct((n_batch, n_rows, d), F32),
        scratch_shapes=[pltpu.VMEM((MOE_TOP_K, TM, d), F32), pltpu.SemaphoreType.DMA(())],
        compiler_params=_cparams(("arbitrary", "arbitrary")),
        name="moe_combine",
    )(slots, x_mid, route, mod_l, y)


def _moe(x_mid, h2, route, counts, mod_l, p, t0):
    n_batch, n_rows, d = h2.shape
    n_tok = n_batch * n_rows
    n_blocks = -(-(n_tok * MOE_TOP_K) // MOE_BM) + MOE_EXPERTS
    cnt = counts[0, :MOE_EXPERTS].astype(I32)
    padded = (cnt + MOE_BM - 1) // MOE_BM * MOE_BM
    pad_end = jnp.cumsum(padded)
    pad_start = pad_end - padded
    blk_expert = jnp.minimum(
        jnp.searchsorted(pad_end, jnp.arange(n_blocks, dtype=I32) * MOE_BM, side='right'), MOE_EXPERTS - 1).astype(I32)
    n_used = (pad_end[-1:] // MOE_BM).astype(I32)
    experts = jnp.arange(MOE_EXPERTS, dtype=I32)

    def slot_of(e, r):
        start = jnp.sum(jnp.where(e[..., None].astype(I32) == experts, pad_start, 0), axis=-1)
        return start + r.astype(I32)

    s1 = slot_of(route[..., R_E1], route[..., R_R1]).reshape(n_batch * n_rows // TM, 1, TM)
    s2 = slot_of(route[..., R_E2], route[..., R_R2]).reshape(n_batch * n_rows // TM, 1, TM)
    slots = jnp.concatenate([s1, s2], axis=-1)
    xs = _dispatch(h2, slots, n_blocks * MOE_BM)
    y = _expert_ffn(xs, blk_expert, n_used, p['w_gu'], p['w_d'])
    return _combine(x_mid, route, slots, y, mod_l, t0)


def _prep_layer(l, w_in, diff_q_g, diff_k_g, diff_lambda, diff_subln_g, ssd_conv_w, ssd_conv_b, ssd_dt_bias,
                ssd_A_log, ssd_D, ssd_norm_g, mla_cq_g, mla_ckv_g, w_uq, w_ukv, mla_q_g, mla_k_g, w_branch, w_out,
                norm1_g, norm2_g, moe_group_w, moe_group_b, moe_expert_w, moe_expert_b, moe_w_gate, moe_w_up,
                moe_w_down):
    d = D_MODEL
    lam_init = 0.8 - 0.6 * math.exp(-0.3 * l)
    wi = w_in[l]
    sizes = (512, 512, 512, 512, 1024, 16, MLA_Q_LORA, MLA_KV_LORA, MLA_ROPE, N_BRANCH * d)
    offs = np.concatenate([[0], np.cumsum(sizes)])
    segs = (SEG_DQ, SEG_DK, SEG_DV, SEG_Z, SEG_XBC, SEG_DT, SEG_CQ, SEG_CKV, SEG_KR, SEG_GATES)
    cols = []
    for i, (st, en) in enumerate(segs):
        piece = wi[:, offs[i]:offs[i + 1]]
        cols.append(jnp.pad(piece, ((0, 0), (0, (en - st) - sizes[i]))))
    p = {'w_in': jnp.concatenate(cols, axis=1).astype(BF16)}
    p['g1'] = norm1_g[l].reshape(1, d)
    p['g2'] = norm2_g[l].reshape(1, d)
    p['dq_g'] = (jnp.tile(diff_q_g[l], 2 * DIFF_HEADS) * (DIFF_DH ** -0.5)).reshape(1, 512)
    p['dk_g'] = jnp.tile(diff_k_g[l], 2 * DIFF_HEADS).reshape(1, 512)
    p['cq_g'] = mla_cq_g[l].reshape(1, MLA_Q_LORA)
    p['ckv_g'] = mla_ckv_g[l].reshape(1, MLA_KV_LORA)
    zpad = jnp.zeros((MLA_HP - MLA_QK,), F32)
    p['mq_g'] = (jnp.tile(jnp.concatenate([mla_q_g[l], zpad]), MLA_HEADS) * (MLA_QK ** -0.5)).reshape(1, -1)
    p['mk_g'] = jnp.tile(jnp.concatenate([mla_k_g[l][:MLA_NOPE], jnp.zeros((MLA_HP - MLA_NOPE,), F32)]),
                         MLA_HEADS).reshape(1, -1)
    p['kr_g'] = jnp.pad(mla_k_g[l][MLA_NOPE:], (0, LANES - MLA_ROPE)).reshape(1, LANES)
    wq = w_uq[l].reshape(MLA_Q_LORA, MLA_HEADS, MLA_QK)
    p['w_uq'] = jnp.pad(wq, ((0, 0), (0, 0), (0, MLA_HP - MLA_QK))).reshape(MLA_Q_LORA, -1).astype(BF16)
    wkv = w_ukv[l].reshape(MLA_KV_LORA, MLA_HEADS, MLA_NOPE + MLA_V)
    p['w_uk'] = jnp.pad(wkv[:, :, :MLA_NOPE], ((0, 0), (0, 0), (0, MLA_HP - MLA_NOPE))).reshape(
        MLA_KV_LORA, -1).astype(BF16)
    p['w_uv'] = wkv[:, :, MLA_NOPE:].reshape(MLA_KV_LORA, -1).astype(BF16)
    p['lam'] = diff_lambda[l]
    p['subln_g'] = (diff_subln_g[l] * (1.0 - lam_init)).reshape(1, DIFF_VD)
    p['lam_init'] = lam_init
    p['conv_w'] = jnp.pad(ssd_conv_w[l], ((0, SUBLANES - SSD_CONV), (0, 0)))
    p['conv_b'] = ssd_conv_b[l].reshape(1, -1)
    p['dt_bias'] = jnp.pad(ssd_dt_bias[l].reshape(-1), (0, LANES - 2 * SSD_HEADS)).reshape(1, LANES)
    p['a_log'] = jnp.pad(ssd_A_log[l].reshape(-1), (0, LANES - 2 * SSD_HEADS)).reshape(1, LANES)
    p['d_skip'] = jnp.repeat(ssd_D[l], SSD_P).reshape(1, SSD_INNER)
    p['ssd_g'] = ssd_norm_g[l].reshape(1, SSD_INNER)
    p['w_branch'] = w_branch[l].astype(BF16)
    p['w_out'] = w_out[l].astype(BF16)
    rw = jnp.pad(jnp.concatenate([moe_group_w[l], moe_expert_w[l]], axis=1),
                 ((0, 0), (0, LANES - MOE_GROUPS - MOE_EXPERTS)))
    p['rw_hi'], p['rw_lo'] = _split2(rw)
    p['rb'] = jnp.pad(jnp.concatenate([moe_group_b[l], moe_expert_b[l]]),
                      (0, LANES - MOE_GROUPS - MOE_EXPERTS)).reshape(1, LANES)
    p['w_gu'] = jnp.concatenate([moe_w_gate[l], moe_w_up[l]], axis=-1).astype(BF16)
    p['w_d'] = moe_w_down[l].astype(BF16)
    return p


def kernel(x, c, ctx, c_ctx, ada_w, ada_b, norm1_g, norm2_g, w_in, diff_q_g, diff_k_g, diff_lambda, diff_subln_g, ssd_conv_w, ssd_conv_b, ssd_dt_bias, ssd_A_log, ssd_D, ssd_norm_g, mla_cq_g, mla_ckv_g, w_uq, w_ukv, mla_q_g, mla_k_g, w_branch, w_out, moe_group_w, moe_group_b, moe_expert_w, moe_expert_b, moe_w_gate, moe_w_up, moe_w_down):
    n_batch, n_lat, d = x.shape
    n_ctx = ctx.shape[1]
    depth = w_in.shape[0]
    assert d == D_MODEL and n_ctx == TM and n_lat % TM == 0 and n_lat % GRID_W == 0
    assert n_batch + 1 <= MOD_ROWS

    cc = jnp.concatenate([c, c_ctx[None, :], jnp.zeros((MOD_ROWS - n_batch - 1, d), F32)], axis=0)
    mod = _modulation(cc, ada_w, ada_b)
    tabs = _static_tables(n_ctx, n_lat)
    xs = jnp.concatenate([ctx, x], axis=1)

    for l in range(depth):
        last = l == depth - 1
        t0 = 1 if last else 0
        p = _prep_layer(l, w_in, diff_q_g, diff_k_g, diff_lambda, diff_subln_g, ssd_conv_w, ssd_conv_b,
                        ssd_dt_bias, ssd_A_log, ssd_D, ssd_norm_g, mla_cq_g, mla_ckv_g, w_uq, w_ukv, mla_q_g,
                        mla_k_g, w_branch, w_out, norm1_g, norm2_g, moe_group_w, moe_group_b, moe_expert_w,
                        moe_expert_b, moe_w_gate, moe_w_up, moe_w_down)
        dq, dk, dv, z, xbc, dt, mq, mk, mv, gates = _in_projection(xs, mod[l], p, tabs)
        yf = _ssd_scan(xbc, dt, p, n_ctx, 0)
        ssd_o = _ssd_scan(xbc, dt, p, n_ctx, 1, yf=yf, z=z)
        diff_o = _diff_attention(dq, dk, dv, p['lam'], p['subln_g'], n_ctx, p['lam_init'], t0)
        mla_o = _mla_attention(mq, mk, mv, n_ctx, t0)
        x_mid, h2, route, counts = _merge(xs, diff_o, ssd_o, mla_o, gates, mod[l], p, t0)
        xs = _moe(x_mid, h2, route, counts, mod[l], p, t0)
    return xs
```

```python
import functools
import math

import jax
import jax.numpy as jnp
import numpy as np
from jax import lax
from jax.experimental import pallas as pl
from jax.experimental.pallas import tpu as pltpu

F32 = jnp.float32
BF16 = jnp.bfloat16
I32 = jnp.int32

D_MODEL = 1024
GRID_W = 64
ROPE_BASE = 10000.0
EPS = 1e-6
DIFF_HEADS = 4
DIFF_DH = 64
DIFF_VD = 2 * DIFF_DH
SSD_HEADS = 8
SSD_P = 64
SSD_INNER = SSD_HEADS * SSD_P
SSD_GROUPS = 2
SSD_STATE = 128
SSD_CONV = 5
SSD_CHUNK = 128
MLA_HEADS = 8
MLA_NOPE = 64
MLA_ROPE = 32
MLA_V = 64
MLA_Q_LORA = 384
MLA_KV_LORA = 256
MLA_QK = MLA_NOPE + MLA_ROPE
N_BRANCH = 3
MOE_GROUPS = 4
MOE_PER_GROUP = 8
MOE_EXPERTS = MOE_GROUPS * MOE_PER_GROUP
MOE_TOP_K = 2
MOE_HIDDEN = 256

LANES = 128
SUBLANES = 8
TM = 256
MOE_BM = 256
ATTN_KEY_CHUNK = 512
VMEM_LIMIT = 56 * 1024 * 1024
HALO = SUBLANES
MOD_ROWS = 16
NEG = -1e30

SEG_DQ = (0, 512)
SEG_DK = (512, 1024)
SEG_DV = (1024, 1536)
SEG_Z = (1536, 2048)
SEG_XBC = (2048, 3072)
SEG_DT = (3072, 3200)
SEG_CQ = (3200, 3584)
SEG_CKV = (3584, 3840)
SEG_KR = (3840, 3968)
SEG_GATES = (3968, 7040)
W_IN_PAD = 7040
MLA_HP = 128

R_E1, R_E2, R_R1, R_R2, R_W1, R_W2 = 0, 1, 2, 3, 4, 5


def _cparams(sem, vmem=VMEM_LIMIT):
    return pltpu.CompilerParams(dimension_semantics=sem, vmem_limit_bytes=vmem)


def _dot(a, b):
    return jnp.dot(a, b, preferred_element_type=F32)


def _dot_nt(a, b):
    return lax.dot_general(a, b, (((1,), (1,)), ((), ())), preferred_element_type=F32)


def _split2(x):
    hi = x.astype(BF16)
    lo = (x - hi.astype(F32)).astype(BF16)
    return hi, lo


def _split3(x):
    hi = x.astype(BF16)
    r = x - hi.astype(F32)
    mid = r.astype(BF16)
    lo = (r - mid.astype(F32)).astype(BF16)
    return hi, mid, lo


def _silu(x):
    return x * jax.nn.sigmoid(x)


def _rms(x):
    return x * lax.rsqrt(jnp.mean(x * x, axis=-1, keepdims=True) + EPS)


def _group_rms_scale(x, gmat, emat):
    ms = _dot((x * x).astype(BF16), gmat)
    r = lax.rsqrt(ms + EPS)
    r_hi, r_lo = _split2(r)
    return x * (_dot(r_hi, emat) + _dot(r_lo, emat))


def _rope(x, cos, sin_a, sin_b, quarter):
    w = x.shape[-1]
    return x * cos + pltpu.roll(x, quarter, 1) * sin_a + pltpu.roll(x, w - quarter, 1) * sin_b


def _const_spec(shape):
    nd = len(shape)
    return pl.BlockSpec(shape, lambda *_: (0,) * nd, pipeline_mode=pl.Buffered(1))


def _mod_kernel(c_ref, w_ref, b_ref, o_ref):
    s = _silu(c_ref[...])
    o_ref[0] = jnp.dot(s, w_ref[0], precision=lax.Precision.HIGHEST, preferred_element_type=F32) + b_ref[0]


def _modulation(cc, ada_w, ada_b):
    depth, d, n = ada_w.shape
    tn = 1536
    return pl.pallas_call(
        _mod_kernel,
        grid=(depth, n // tn),
        in_specs=[pl.BlockSpec((MOD_ROWS, d), lambda l, j: (0, 0)),
                  pl.BlockSpec((1, d, tn), lambda l, j: (l, 0, j)),
                  pl.BlockSpec((1, 1, tn), lambda l, j: (l, 0, j))],
        out_specs=pl.BlockSpec((1, MOD_ROWS, tn), lambda l, j: (l, 0, j)),
        out_shape=jax.ShapeDtypeStruct((depth, MOD_ROWS, n), F32),
        compiler_params=_cparams(("arbitrary", "arbitrary")),
        name="adaln_mod",
    )(cc, ada_w, ada_b.reshape(depth, 1, n))


def _inproj_kernel(x_ref, mod_ref, g1_ref, w_ref, wuq_ref, wuk_ref, wuv_ref,
                   gd_ref, ed_ref, gq_ref, eq_ref, gk_ref, ek_ref, ekr_ref,
                   cosd_ref, sad_ref, sbd_ref, cosq_ref, saq_ref, sbq_ref, cosk_ref, sak_ref, sbk_ref,
                   dqg_ref, dkg_ref, cqg_ref, ckvg_ref, mqg_ref, mkg_ref, krg_ref, mvo_ref,
                   dq_ref, dk_ref, dv_ref, z_ref, xbc_ref, dt_ref, mq_ref, mk_ref, mv_ref, gates_ref,
                   *, n_batch):
    t = pl.program_id(0)
    b = pl.program_id(1)
    row = jnp.where(t == 0, n_batch, b)
    shift = mod_ref[pl.ds(row, 1), 0:D_MODEL]
    scale = mod_ref[pl.ds(row, 1), D_MODEL:2 * D_MODEL]
    xn = _rms(x_ref[0]) * g1_ref[...]
    h = (xn * (1.0 + scale) + shift).astype(BF16)

    def proj(seg):
        return _dot(h, w_ref[:, seg[0]:seg[1]])

    cosd, sad, sbd = cosd_ref[...], sad_ref[...], sbd_ref[...]
    dq = _group_rms_scale(proj(SEG_DQ), gd_ref[...], ed_ref[...]) * dqg_ref[...]
    dq_ref[0] = _rope(dq, cosd, sad, sbd, DIFF_DH // 4).astype(BF16)
    dk = _group_rms_scale(proj(SEG_DK), gd_ref[...], ed_ref[...]) * dkg_ref[...]
    dk_ref[0] = _rope(dk, cosd, sad, sbd, DIFF_DH // 4).astype(BF16)
    dv_ref[0] = proj(SEG_DV).astype(BF16)
    z_ref[0] = proj(SEG_Z).astype(BF16)
    xbc_ref[0] = proj(SEG_XBC)
    dt_ref[0] = proj(SEG_DT)
    gates_ref[0] = jax.nn.sigmoid(proj(SEG_GATES)).astype(BF16)

    cq = _rms(proj(SEG_CQ)) * cqg_ref[...]
    q = _dot(cq.astype(BF16), wuq_ref[...])
    q = _group_rms_scale(q, gq_ref[...], eq_ref[...]) * mqg_ref[...]
    mq_ref[0] = _rope(q, cosq_ref[...], saq_ref[...], sbq_ref[...], MLA_ROPE // 4).astype(BF16)

    ckv = (_rms(proj(SEG_CKV)) * ckvg_ref[...]).astype(BF16)
    kn = _dot(ckv, wuk_ref[...])
    kn = _group_rms_scale(kn, gk_ref[...], ek_ref[...]) * mkg_ref[...]
    mv_ref[0] = (_dot(ckv, wuv_ref[...]) + mvo_ref[...]).astype(BF16)
    kr = proj(SEG_KR)
    kr = kr * lax.rsqrt(jnp.sum(kr * kr, axis=-1, keepdims=True) * (1.0 / MLA_ROPE) + EPS) * krg_ref[...]
    kr = _rope(kr, cosk_ref[...], sak_ref[...], sbk_ref[...], MLA_ROPE // 4).astype(BF16)
    mk_ref[0] = (kn + _dot(kr, ekr_ref[...])).astype(BF16)


def _rope_tables(n_ctx, n_lat, dim, lane_layout):
    quarter = dim // 4
    inv_freq = ROPE_BASE ** (-jnp.arange(quarter, dtype=F32) / quarter)
    s = jnp.arange(n_lat, dtype=I32)
    ar = (s // GRID_W).astype(F32)[:, None] * inv_freq
    ac = (s % GRID_W).astype(F32)[:, None] * inv_freq
    ang = jnp.concatenate([ar, ar, ac, ac], axis=-1)
    ang = jnp.concatenate([jnp.zeros((n_ctx, dim), F32), ang], axis=0)
    odd = jnp.asarray((np.arange(dim) // quarter) % 2 == 1)
    cos_d, sin_d = jnp.cos(ang), jnp.sin(ang)
    units = (cos_d, jnp.where(odd, sin_d, 0.0), jnp.where(odd, 0.0, -sin_d))
    out = []
    for unit, fill in zip(units, (1.0, 0.0, 0.0)):
        pieces = [unit if item == 'rope' else jnp.full((n_ctx + n_lat, item), fill, F32) for item in lane_layout]
        out.append(jnp.concatenate(pieces, axis=-1))
    return out


def _group_mats(width, groups):
    g = np.zeros((width, LANES), np.float32)
    e = np.zeros((LANES, width), np.float32)
    for i, (st, sz) in enumerate(groups):
        g[st:st + sz, i] = 1.0 / sz
        e[i, st:st + sz] = 1.0
    return jnp.asarray(g, BF16), jnp.asarray(e, BF16)


def _static_tables(n_ctx, n_lat):
    tabs = {}
    tabs['gd'], tabs['ed'] = _group_mats(512, [(i * DIFF_DH, DIFF_DH) for i in range(2 * DIFF_HEADS)])
    qgroups = []
    for h in range(MLA_HEADS):
        qgroups += [(h * MLA_HP, MLA_NOPE), (h * MLA_HP + MLA_NOPE, MLA_ROPE)]
    tabs['gq'], tabs['eq'] = _group_mats(MLA_HEADS * MLA_HP, qgroups)
    tabs['gk'], tabs['ek'] = _group_mats(MLA_HEADS * MLA_HP, [(h * MLA_HP, MLA_NOPE) for h in range(MLA_HEADS)])
    ekr = np.zeros((LANES, MLA_HEADS * MLA_HP), np.float32)
    for h in range(MLA_HEADS):
        for j in range(MLA_ROPE):
            ekr[j, h * MLA_HP + MLA_NOPE + j] = 1.0
    tabs['ekr'] = jnp.asarray(ekr, BF16)
    mv_ones = np.zeros((1, MLA_HEADS * MLA_HP), np.float32)
    for h in range(MLA_HEADS):
        mv_ones[0, h * MLA_HP + (MLA_V if h % 2 == 0 else 0)] = 1.0
    tabs['mv_ones'] = jnp.asarray(mv_ones)
    tabs['cosd'], tabs['sad'], tabs['sbd'] = _rope_tables(n_ctx, n_lat, DIFF_DH, ['rope'] * (2 * DIFF_HEADS))
    tabs['cosq'], tabs['saq'], tabs['sbq'] = _rope_tables(
        n_ctx, n_lat, MLA_ROPE, [MLA_NOPE, 'rope', MLA_HP - MLA_QK] * MLA_HEADS)
    tabs['cosk'], tabs['sak'], tabs['sbk'] = _rope_tables(n_ctx, n_lat, MLA_ROPE, ['rope', LANES - MLA_ROPE])
    return tabs


def _in_projection(xs, mod_l, p, tabs):
    n_batch, n_rows, d = xs.shape
    nt = n_rows // TM
    row_spec = lambda w: pl.BlockSpec((1, TM, w), lambda t, b: (b, t, 0))
    tab_spec = lambda w: pl.BlockSpec((TM, w), lambda t, b: (t, 0))
    consts = [mod_l, p['g1'], p['w_in'], p['w_uq'], p['w_uk'], p['w_uv'],
              tabs['gd'], tabs['ed'], tabs['gq'], tabs['eq'], tabs['gk'], tabs['ek'], tabs['ekr']]
    rope = [tabs['cosd'], tabs['sad'], tabs['sbd'], tabs['cosq'], tabs['saq'], tabs['sbq'],
            tabs['cosk'], tabs['sak'], tabs['sbk']]
    gains = [p['dq_g'], p['dk_g'], p['cq_g'], p['ckv_g'], p['mq_g'], p['mk_g'], p['kr_g'], tabs['mv_ones']]
    out_w = [(512, BF16), (512, BF16), (512, BF16), (512, BF16), (1024, F32), (LANES, F32),
             (MLA_HEADS * MLA_HP, BF16), (MLA_HEADS * MLA_HP, BF16), (MLA_HEADS * MLA_HP, BF16),
             (N_BRANCH * D_MODEL, BF16)]
    return pl.pallas_call(
        functools.partial(_inproj_kernel, n_batch=n_batch),
        grid=(nt, n_batch),
        in_specs=([row_spec(d)] + [_const_spec(a.shape) for a in consts]
                  + [tab_spec(a.shape[1]) for a in rope] + [_const_spec(a.shape) for a in gains]),
        out_specs=[row_spec(w) for w, _ in out_w],
        out_shape=[jax.ShapeDtypeStruct((n_batch, n_rows, w), dt) for w, dt in out_w],
        compiler_params=_cparams(("arbitrary", "arbitrary")),
        name="in_proj",
    )(xs, *consts, *rope, *gains)


def _ssd_chunk_of_step(s, direction, n_ctx_chunks, n_chunks):
    if direction == 0:
        return s
    return jnp.where(s < n_ctx_chunks, n_ctx_chunks - 1 - s, n_chunks - 1 + n_ctx_chunks - s)


def _ssd_kernel(*refs, direction, n_ctx_chunks, n_chunks):
    if direction == 0:
        (xbc_ref, prev_ref, next_ref, dt_ref, cw_ref, cb_ref, dtb_ref, alog_ref, y_ref, state_ref) = refs
    else:
        (xbc_ref, prev_ref, next_ref, dt_ref, cw_ref, cb_ref, dtb_ref, alog_ref,
         yf_ref, z_ref, dskip_ref, ng_ref, y_ref, state_ref) = refs
    s = pl.program_id(1)
    c = _ssd_chunk_of_step(s, direction, n_ctx_chunks, n_chunks)
    ck = SSD_CHUNK

    @pl.when(s == 0)
    def _():
        state_ref[...] = jnp.zeros_like(state_ref)

    first = jnp.logical_or(c == 0, c == n_ctx_chunks)
    last = jnp.logical_or(c == n_ctx_chunks - 1, c == n_chunks - 1)
    prev = jnp.where(first, 0.0, prev_ref[0])
    nxt = jnp.where(last, 0.0, next_ref[0])
    full = jnp.concatenate([prev, xbc_ref[0], nxt], axis=0)
    n_full = ck + 2 * HALO
    half = (SSD_CONV - 1) // 2
    conv = None
    for k in range(SSD_CONV):
        sh = (half - k) % n_full
        rolled = full if sh == 0 else pltpu.roll(full, sh, 0)
        term = rolled[HALO:HALO + ck] * cw_ref[k:k + 1, :]
        conv = term if conv is None else conv + term
    u = _silu(conv + cb_ref[...])
    xs = u[:, :SSD_INNER]
    bm = u[:, SSD_INNER:SSD_INNER + SSD_GROUPS * SSD_STATE]
    cm = u[:, SSD_INNER + SSD_GROUPS * SSD_STATE:]

    dt = jax.nn.softplus(dt_ref[0] + dtb_ref[...])
    a = dt * (-jnp.exp(alog_ref[...]))
    ri = lax.broadcasted_iota(I32, (ck, ck), 0)
    ci = lax.broadcasted_iota(I32, (ck, ck), 1)
    valid = (ci <= ri) if direction == 0 else (ci >= ri)
    tri = jnp.where(valid, 1.0, 0.0).astype(BF16)
    a1, a2, a3 = _split3(a)
    acum = _dot(tri, a1) + _dot(tri, a2) + _dot(tri, a3)
    end_row = ck - 1 if direction == 0 else 0
    atot = acum[end_row:end_row + 1, :]
    w_end = jnp.exp(atot - acum) * dt
    e_in = jnp.exp(acum)
    cdec = jnp.exp(atot)
    acum_t = acum.T
    dt_t = dt.T
    lane = lax.broadcasted_iota(I32, (1, LANES), 1)
    left = lane < SSD_P

    pairs_per_group = SSD_HEADS // SSD_GROUPS // 2
    ys = []
    for g in range(SSD_GROUPS):
        bg = bm[:, g * SSD_STATE:(g + 1) * SSD_STATE]
        cg = cm[:, g * SSD_STATE:(g + 1) * SSD_STATE].astype(BF16)
        cb = _dot_nt(cg, bg.astype(BF16))
        bg_t = bg.T.astype(BF16)
        for pp in range(pairs_per_group):
            pr = g * pairs_per_group + pp
            xs_p = xs[:, pr * LANES:(pr + 1) * LANES]
            c0 = direction * SSD_HEADS + 2 * pr
            yd = None
            for side in range(2):
                col = c0 + side
                seg = acum[:, col:col + 1] - acum_t[col:col + 1, :]
                decay = jnp.where(valid, jnp.exp(jnp.where(valid, seg, 0.0)), 0.0)
                mix = (cb * decay * dt_t[col:col + 1, :]).astype(BF16)
                keep = left if side == 0 else jnp.logical_not(left)
                part = _dot(mix, jnp.where(keep, xs_p, 0.0).astype(BF16))
                yd = part if yd is None else yd + part
            pick = lambda m: jnp.where(left, m[:, c0:c0 + 1], m[:, c0 + 1:c0 + 2])
            h_in = state_ref[pr]
            y_off = _dot(cg, h_in.astype(BF16)) * pick(e_in)
            xw = (xs_p * pick(w_end)).astype(BF16)
            state_ref[pr] = h_in * pick(cdec) + _dot(bg_t, xw)
            ys.append(yd + y_off)
    y = jnp.concatenate(ys, axis=-1)
    if direction == 0:
        y_ref[0] = y
    else:
        y = y + yf_ref[0] + dskip_ref[...] * xs
        y = y * _silu(z_ref[0].astype(F32))
        y_ref[0] = (_rms(y) * ng_ref[...]).astype(BF16)


def _ssd_scan(xbc, dt, p, n_ctx, direction, yf=None, z=None):
    n_batch, n_rows, wc = xbc.shape
    ck = SSD_CHUNK
    n_chunks = n_rows // ck
    n_ctx_chunks = n_ctx // ck
    n_halo_blocks = n_rows // HALO
    per = ck // HALO
    cmap = lambda s: _ssd_chunk_of_step(s, direction, n_ctx_chunks, n_chunks)
    cur = lambda w: pl.BlockSpec((1, ck, w), lambda b, s: (b, cmap(s), 0))
    in_specs = [cur(wc),
                pl.BlockSpec((1, HALO, wc), lambda b, s: (b, jnp.maximum(cmap(s) * per - 1, 0), 0)),
                pl.BlockSpec((1, HALO, wc), lambda b, s: (b, jnp.minimum((cmap(s) + 1) * per, n_halo_blocks - 1), 0)),
                cur(LANES),
                _const_spec(p['conv_w'].shape), _const_spec(p['conv_b'].shape),
                _const_spec(p['dt_bias'].shape), _const_spec(p['a_log'].shape)]
    args = [xbc, xbc, xbc, dt, p['conv_w'], p['conv_b'], p['dt_bias'], p['a_log']]
    if direction == 0:
        out_dtype = F32
    else:
        in_specs += [cur(SSD_INNER), cur(SSD_INNER), _const_spec(p['d_skip'].shape), _const_spec(p['ssd_g'].shape)]
        args += [yf, z, p['d_skip'], p['ssd_g']]
        out_dtype = BF16
    return pl.pallas_call(
        functools.partial(_ssd_kernel, direction=direction, n_ctx_chunks=n_ctx_chunks, n_chunks=n_chunks),
        grid=(n_batch, n_chunks),
        in_specs=in_specs,
        out_specs=cur(SSD_INNER),
        out_shape=jax.ShapeDtypeStruct((n_batch, n_rows, SSD_INNER), out_dtype),
        scratch_shapes=[pltpu.VMEM((SSD_HEADS // 2, SSD_STATE, LANES), F32)],
        compiler_params=_cparams(("arbitrary", "arbitrary")),
        name="ssd_fwd" if direction == 0 else "ssd_bwd",
    )(*args)


def _softmax_parts(s):
    m = jnp.max(s, axis=-1, keepdims=True)
    e = jnp.exp(s - m)
    return e, 1.0 / jnp.sum(e, axis=-1, keepdims=True)


def _softmax_numerator(s):
    m = jnp.max(s, axis=-1, keepdims=True)
    return jnp.exp((s - m).astype(BF16))


def _diff_attn_kernel(q_ref, k_ref, v_ref, lam_ref, sg_ref, o_ref, *, n_ctx, lam_init, t0):
    lv = lam_ref[...]
    lam = (jnp.exp(jnp.sum(lv[0:1] * lv[1:2], axis=-1, keepdims=True))
           - jnp.exp(jnp.sum(lv[2:3] * lv[3:4], axis=-1, keepdims=True)) + lam_init)
    lane = lax.broadcasted_iota(I32, (1, DIFF_VD), 1)
    q = q_ref[0]
    q0 = jnp.where(lane < DIFF_DH, q, jnp.zeros_like(q))
    q1 = jnp.where(lane < DIFF_DH, jnp.zeros_like(q), q)

    def attend(n_keys):
        k = k_ref[0, 0:n_keys, :]
        e0, r0 = _softmax_parts(_dot_nt(q0, k))
        e1, r1 = _softmax_parts(_dot_nt(q1, k))
        w = (e0 * r0 - e1 * (lam * r1)).astype(BF16)
        o = _dot(w, v_ref[0, 0:n_keys, :])
        o_ref[0] = (_rms(o) * sg_ref[...]).astype(BF16)

    is_ctx = t0 + pl.program_id(2) == 0

    @pl.when(is_ctx)
    def _():
        attend(n_ctx)

    @pl.when(jnp.logical_not(is_ctx))
    def _():
        attend(k_ref.shape[1])


def _diff_attention(dq, dk, dv, lam_p, subln_g, n_ctx, lam_init, t0):
    n_batch, n_rows, _ = dq.shape
    nq = n_rows // TM - t0
    return pl.pallas_call(
        functools.partial(_diff_attn_kernel, n_ctx=n_ctx, lam_init=lam_init, t0=t0),
        grid=(n_batch, DIFF_HEADS, nq),
        in_specs=[pl.BlockSpec((1, TM, DIFF_VD), lambda b, h, t: (b, t0 + t, h)),
                  pl.BlockSpec((1, n_rows, DIFF_VD), lambda b, h, t: (b, 0, h)),
                  pl.BlockSpec((1, n_rows, DIFF_VD), lambda b, h, t: (b, 0, h)),
                  _const_spec(lam_p.shape), _const_spec(subln_g.shape)],
        out_specs=pl.BlockSpec((1, TM, DIFF_VD), lambda b, h, t: (b, t, h)),
        out_shape=jax.ShapeDtypeStruct((n_batch, nq * TM, DIFF_HEADS * DIFF_VD), BF16),
        compiler_params=_cparams(("arbitrary", "arbitrary", "arbitrary")),
        name="diff_attn",
    )(dq, dk, dv, lam_p, subln_g)


def _mla_attn_kernel(q_ref, k_ref, v_ref, o_ref, *, n_ctx, t0):
    lane = lax.broadcasted_iota(I32, (1, MLA_HP), 1)

    def attend(n_keys):
        res = []
        for side in range(2):
            sl = slice(side * MLA_HP, (side + 1) * MLA_HP)
            e = _softmax_numerator(_dot_nt(q_ref[0, :, sl], k_ref[0, 0:n_keys, sl]))
            res.append(_dot(e, v_ref[0, 0:n_keys, sl]))
        o0 = res[0] * (1.0 / res[0][:, MLA_V:MLA_V + 1])
        o1 = res[1] * (1.0 / res[1][:, 0:1])
        o_ref[0] = jnp.where(lane < MLA_V, o0, o1).astype(BF16)

    is_ctx = t0 + pl.program_id(2) == 0

    @pl.when(is_ctx)
    def _():
        attend(n_ctx)

    @pl.when(jnp.logical_not(is_ctx))
    def _():
        attend(k_ref.shape[1])


def _mla_attention(mq, mk, mv, n_ctx, t0):
    n_batch, n_rows, _ = mq.shape
    nq = n_rows // TM - t0
    return pl.pallas_call(
        functools.partial(_mla_attn_kernel, n_ctx=n_ctx, t0=t0),
        grid=(n_batch, MLA_HEADS // 2, nq),
        in_specs=[pl.BlockSpec((1, TM, 2 * MLA_HP), lambda b, h, t: (b, t0 + t, h)),
                  pl.BlockSpec((1, n_rows, 2 * MLA_HP), lambda b, h, t: (b, 0, h)),
                  pl.BlockSpec((1, n_rows, 2 * MLA_HP), lambda b, h, t: (b, 0, h))],
        out_specs=pl.BlockSpec((1, TM, 2 * MLA_V), lambda b, h, t: (b, t, h)),
        out_shape=jax.ShapeDtypeStruct((n_batch, nq * TM, MLA_HEADS * MLA_V), BF16),
        compiler_params=_cparams(("arbitrary", "arbitrary", "arbitrary")),
        name="mla_attn",
    )(mq, mk, mv)


def _merge_kernel(x_ref, da_ref, ss_ref, ma_ref, gates_ref, mod_ref, wb_ref, wo_ref, g2_ref,
                  rwh_ref, rwl_ref, rb_ref,
                  xo_ref, h2_ref, route_ref, cnt_ref, base_ref, *, n_batch, t0):
    b = pl.program_id(0)
    t = pl.program_id(1)

    @pl.when(jnp.logical_and(b == 0, t == 0))
    def _():
        base_ref[...] = jnp.zeros_like(base_ref)

    row = jnp.where(t0 + t == 0, n_batch, b)
    gate1 = mod_ref[pl.ds(row, 1), 2 * D_MODEL:3 * D_MODEL]
    shift2 = mod_ref[pl.ds(row, 1), 3 * D_MODEL:4 * D_MODEL]
    scale2 = mod_ref[pl.ds(row, 1), 4 * D_MODEL:5 * D_MODEL]

    acc = None
    for k, o_ref in enumerate((da_ref, ss_ref, ma_ref)):
        term = gates_ref[0, :, k * D_MODEL:(k + 1) * D_MODEL].astype(F32) * _dot(o_ref[0], wb_ref[k])
        acc = term if acc is None else acc + term
    x = x_ref[0] + gate1 * _dot(acc.astype(BF16), wo_ref[...])
    xo_ref[0] = x
    h2 = _rms(x) * g2_ref[...] * (1.0 + scale2) + shift2
    h2_ref[0] = h2

    h_hi, h_lo = _split2(h2)
    lg = _dot(h_hi, rwh_ref[...]) + _dot(h_lo, rwh_ref[...]) + _dot(h_hi, rwl_ref[...]) + rb_ref[...]
    lane = lax.broadcasted_iota(I32, lg.shape, 1)
    lane_f = lane.astype(F32)
    big = float(LANES)
    is_g = lane < MOE_GROUPS
    gl = jnp.where(is_g, lg, NEG)
    gmax = jnp.max(gl, axis=-1, keepdims=True)
    gidx = jnp.min(jnp.where(gl == gmax, lane_f, big), axis=-1, keepdims=True)
    pg = 1.0 / jnp.sum(jnp.where(is_g, jnp.exp(gl - gmax), 0.0), axis=-1, keepdims=True)
    lo_lane = MOE_GROUPS + gidx * MOE_PER_GROUP
    in_grp = jnp.logical_and(lane_f >= lo_lane, lane_f < lo_lane + MOE_PER_GROUP)
    el = jnp.where(in_grp, lg, NEG)
    m1 = jnp.max(el, axis=-1, keepdims=True)
    i1 = jnp.min(jnp.where(el == m1, lane_f, big), axis=-1, keepdims=True)
    el2 = jnp.where(lane_f == i1, NEG, el)
    m2 = jnp.max(el2, axis=-1, keepdims=True)
    i2 = jnp.min(jnp.where(el2 == m2, lane_f, big), axis=-1, keepdims=True)
    tt = jnp.exp(m2 - m1)
    w1 = pg / (1.0 + tt)
    w2 = pg * tt / (1.0 + tt)
    e1 = i1 - MOE_GROUPS
    e2 = i2 - MOE_GROUPS

    oh1 = lane_f == e1
    oh2 = lane_f == e2
    oh = jnp.where(jnp.logical_or(oh1, oh2), 1.0, 0.0)
    tm = oh.shape[0]
    ri = lax.broadcasted_iota(I32, (tm, tm), 0)
    ci = lax.broadcasted_iota(I32, (tm, tm), 1)
    strict = jnp.where(ci < ri, 1.0, 0.0).astype(BF16)
    base = base_ref[0:1, :]
    rank_all = _dot(strict, oh.astype(BF16)) + base
    r1 = jnp.sum(jnp.where(oh1, rank_all, 0.0), axis=-1, keepdims=True)
    r2 = jnp.sum(jnp.where(oh2, rank_all, 0.0), axis=-1, keepdims=True)
    new_base = base + jnp.sum(oh, axis=0, keepdims=True)
    base_ref[...] = jnp.broadcast_to(new_base, base_ref.shape)
    cnt_ref[...] = jnp.broadcast_to(new_base, cnt_ref.shape)

    rec = jnp.zeros(lg.shape, F32)
    for ln, val in ((R_E1, e1), (R_E2, e2), (R_R1, r1), (R_R2, r2), (R_W1, w1), (R_W2, w2)):
        rec = jnp.where(lane == ln, val, rec)
    route_ref[0] = rec


def _merge(xs, da, ss, ma, gates, mod_l, p, t0):
    n_batch, n_rows, d = xs.shape
    nt = n_rows // TM - t0
    rows_out = nt * TM
    in_row = lambda w: pl.BlockSpec((1, TM, w), lambda b, t: (b, t0 + t, 0))
    out_row = lambda w: pl.BlockSpec((1, TM, w), lambda b, t: (b, t, 0))
    consts = [mod_l, p['w_branch'], p['w_out'], p['g2'], p['rw_hi'], p['rw_lo'], p['rb']]
    return pl.pallas_call(
        functools.partial(_merge_kernel, n_batch=n_batch, t0=t0),
        grid=(n_batch, nt),
        in_specs=[in_row(d), out_row(512), in_row(512), out_row(512), in_row(N_BRANCH * D_MODEL)]
                 + [_const_spec(a.shape) for a in consts],
        out_specs=[out_row(d), out_row(d), out_row(LANES), pl.BlockSpec((SUBLANES, LANES), lambda b, t: (0, 0))],
        out_shape=[jax.ShapeDtypeStruct((n_batch, rows_out, d), F32),
                   jax.ShapeDtypeStruct((n_batch, rows_out, d), F32),
                   jax.ShapeDtypeStruct((n_batch, rows_out, LANES), F32),
                   jax.ShapeDtypeStruct((SUBLANES, LANES), F32)],
        scratch_shapes=[pltpu.VMEM((SUBLANES, LANES), F32)],
        compiler_params=_cparams(("arbitrary", "arbitrary")),
        name="merge_route",
    )(xs, da, ss, ma, gates, *consts)


def _row_copy(src_ref, src_row, dst_ref, dst_row, sem):
    return pltpu.make_async_copy(src_ref.at[pl.ds(src_row, 1), :], dst_ref.at[pl.ds(dst_row, 1), :], sem)


def _slot(seg_ref, ids_ref, which, r, tm):
    return seg_ref[ids_ref[0, 0, which * tm + r]] + ids_ref[0, 0, (MOE_TOP_K + which) * tm + r]


def _dispatch_kernel(seg_ref, ids_ref, h2_ref, xs_ref, zero_ref, sem, zsem):
    tm = h2_ref.shape[1]
    src = h2_ref.at[0]

    @pl.when(jnp.logical_and(pl.program_id(0) == 0, pl.program_id(1) == 0))
    def _():
        zero_ref[...] = jnp.zeros_like(zero_ref)

        def zero_tail(e, carry):
            start = seg_ref[e]
            end = seg_ref[e + 1]

            @pl.when(end > start)
            def _():
                cp = pltpu.make_async_copy(
                    zero_ref, xs_ref.at[pl.ds(pl.multiple_of(end - MOE_BM, MOE_BM), MOE_BM), :], zsem)
                cp.start()
                cp.wait()
            return carry

        lax.fori_loop(0, MOE_EXPERTS, zero_tail, 0)

        def zero_unused(j, carry):
            cp = pltpu.make_async_copy(zero_ref, xs_ref.at[pl.ds(pl.multiple_of(j * MOE_BM, MOE_BM), MOE_BM), :], zsem)
            cp.start()
            cp.wait()
            return carry

        lax.fori_loop(seg_ref[MOE_EXPERTS] // MOE_BM, xs_ref.shape[0] // MOE_BM, zero_unused, 0)

    def issue(r, carry):
        for which in range(MOE_TOP_K):
            _row_copy(src, r, xs_ref, _slot(seg_ref, ids_ref, which, r, tm), sem).start()
        return carry

    lax.fori_loop(0, tm, issue, 0, unroll=8)
    for _ in range(MOE_TOP_K):
        pltpu.make_async_copy(src, xs_ref.at[pl.ds(0, tm), :], sem).wait()


def _dispatch(h2, seg_start, ids, n_slots):
    n_batch, n_rows, d = h2.shape
    nt = n_rows // TM
    grid_spec = pltpu.PrefetchScalarGridSpec(
        num_scalar_prefetch=1,
        grid=(n_batch, nt),
        in_specs=[pl.BlockSpec((1, 1, 2 * MOE_TOP_K * TM), lambda b, t, seg: (b * nt + t, 0, 0),
                               memory_space=pltpu.SMEM),
                  pl.BlockSpec((1, TM, d), lambda b, t, seg: (b, t, 0))],
        out_specs=pl.BlockSpec(memory_space=pl.ANY),
        scratch_shapes=[pltpu.VMEM((MOE_BM, d), F32), pltpu.SemaphoreType.DMA(()), pltpu.SemaphoreType.DMA(())],
    )
    return pl.pallas_call(
        _dispatch_kernel,
        grid_spec=grid_spec,
        out_shape=jax.ShapeDtypeStruct((n_slots, d), F32),
        compiler_params=_cparams(("arbitrary", "arbitrary")),
        name="moe_dispatch",
    )(seg_start, ids, h2)


def _expert_kernel(be_ref, nu_ref, xs_ref, wgu_ref, wd_ref, y_ref):
    i = pl.program_id(0)

    @pl.when(i < nu_ref[0])
    def _():
        gu = _dot(xs_ref[...].astype(BF16), wgu_ref[0])
        hid = _silu(gu[:, :MOE_HIDDEN]) * gu[:, MOE_HIDDEN:]
        y_ref[...] = _dot(hid.astype(BF16), wd_ref[0])

    @pl.when(i >= nu_ref[0])
    def _():
        y_ref[...] = jnp.zeros_like(y_ref)


def _expert_ffn(xs, blk_expert, n_used, w_gu, w_d):
    n_slots, d = xs.shape
    n_blocks = n_slots // MOE_BM
    grid_spec = pltpu.PrefetchScalarGridSpec(
        num_scalar_prefetch=2,
        grid=(n_blocks,),
        in_specs=[pl.BlockSpec((MOE_BM, d), lambda i, be, nu: (jnp.minimum(i, nu[0] - 1), 0)),
                  pl.BlockSpec((1, d, 2 * MOE_HIDDEN), lambda i, be, nu: (be[i], 0, 0)),
                  pl.BlockSpec((1, MOE_HIDDEN, d), lambda i, be, nu: (be[i], 0, 0))],
        out_specs=pl.BlockSpec((MOE_BM, d), lambda i, be, nu: (i, 0)),
    )
    return pl.pallas_call(
        _expert_kernel,
        grid_spec=grid_spec,
        out_shape=jax.ShapeDtypeStruct((n_slots, d), F32),
        compiler_params=_cparams(("arbitrary",)),
        name="moe_experts",
    )(blk_expert, n_used, xs, w_gu, w_d)


def _combine_kernel(seg_ref, ids_ref, x_ref, route_ref, mod_ref, y_ref, o_ref, buf_ref, sem, *, n_batch, t0):
    b = pl.program_id(0)
    t = pl.program_id(1)
    tm = x_ref.shape[1]

    def issue(r, carry):
        for which in range(MOE_TOP_K):
            _row_copy(y_ref, _slot(seg_ref, ids_ref, which, r, tm), buf_ref.at[which], r, sem).start()
        return carry

    lax.fori_loop(0, tm, issue, 0, unroll=8)
    for which in range(MOE_TOP_K):
        pltpu.make_async_copy(y_ref.at[pl.ds(0, tm), :], buf_ref.at[which], sem).wait()

    row = jnp.where(t0 + t == 0, n_batch, b)
    gate2 = mod_ref[pl.ds(row, 1), 5 * D_MODEL:6 * D_MODEL]
    rec = route_ref[0]
    f = rec[:, R_W1:R_W1 + 1] * buf_ref[0] + rec[:, R_W2:R_W2 + 1] * buf_ref[1]
    o_ref[0] = x_ref[0] + gate2 * f


def _combine(x_mid, route, seg_start, ids, y, mod_l, t0):
    n_batch, n_rows, d = x_mid.shape
    nt = n_rows // TM
    row = lambda w: pl.BlockSpec((1, TM, w), lambda b, t, seg: (b, t, 0))
    grid_spec = pltpu.PrefetchScalarGridSpec(
        num_scalar_prefetch=1,
        grid=(n_batch, nt),
        in_specs=[pl.BlockSpec((1, 1, 2 * MOE_TOP_K * TM), lambda b, t, seg: (b * nt + t, 0, 0),
                               memory_space=pltpu.SMEM),
                  row(d), row(LANES),
                  pl.BlockSpec(mod_l.shape, lambda b, t, seg: (0, 0), pipeline_mode=pl.Buffered(1)),
                  pl.BlockSpec(memory_space=pl.ANY)],
        out_specs=row(d),
        scratch_shapes=[pltpu.VMEM((MOE_TOP_K, TM, d), F32), pltpu.SemaphoreType.DMA(())],
    )
    return pl.pallas_call(
        functools.partial(_combine_kernel, n_batch=n_batch, t0=t0),
        grid_spec=grid_spec,
        out_shape=jax.ShapeDtypeStruct((n_batch, n_rows, d), F32),
        compiler_params=_cparams(("arbitrary", "arbitrary")),
        name="moe_combine",
    )(seg_start, ids, x_mid, route, mod_l, y)


def _moe(x_mid, h2, route, counts, mod_l, p, t0):
    n_batch, n_rows, d = h2.shape
    n_tok = n_batch * n_rows
    n_blocks = -(-(n_tok * MOE_TOP_K) // MOE_BM) + MOE_EXPERTS
    cnt = counts[0, :MOE_EXPERTS].astype(I32)
    padded = (cnt + MOE_BM - 1) // MOE_BM * MOE_BM
    pad_end = jnp.cumsum(padded)
    seg_start = jnp.concatenate([jnp.zeros((1,), I32), pad_end]).astype(I32)
    blk_first = jnp.arange(n_blocks, dtype=I32) * MOE_BM
    blk_expert = jnp.minimum(jnp.sum((pad_end[None, :] <= blk_first[:, None]).astype(I32), axis=1), MOE_EXPERTS - 1)
    n_used = (pad_end[-1:] // MOE_BM).astype(I32)
    ids = route[..., R_E1:R_R2 + 1].astype(I32).reshape(n_tok // TM, TM, 2 * MOE_TOP_K)
    ids = jnp.swapaxes(ids, 1, 2).reshape(n_tok // TM, 1, 2 * MOE_TOP_K * TM)
    xs = _dispatch(h2, seg_start, ids, n_blocks * MOE_BM)
    y = _expert_ffn(xs, blk_expert, n_used, p['w_gu'], p['w_d'])
    return _combine(x_mid, route, seg_start, ids, y, mod_l, t0)


def _prep_layer(l, w_in, diff_q_g, diff_k_g, diff_lambda, diff_subln_g, ssd_conv_w, ssd_conv_b, ssd_dt_bias,
                ssd_A_log, ssd_D, ssd_norm_g, mla_cq_g, mla_ckv_g, w_uq, w_ukv, mla_q_g, mla_k_g, w_branch, w_out,
                norm1_g, norm2_g, moe_group_w, moe_group_b, moe_expert_w, moe_expert_b, moe_w_gate, moe_w_up,
                moe_w_down):
    d = D_MODEL
    lam_init = 0.8 - 0.6 * math.exp(-0.3 * l)
    wi = w_in[l]
    sizes = (512, 512, 512, 512, 1024, 16, MLA_Q_LORA, MLA_KV_LORA, MLA_ROPE, N_BRANCH * d)
    offs = np.concatenate([[0], np.cumsum(sizes)])
    segs = (SEG_DQ, SEG_DK, SEG_DV, SEG_Z, SEG_XBC, SEG_DT, SEG_CQ, SEG_CKV, SEG_KR, SEG_GATES)
    cols = []
    for i, (st, en) in enumerate(segs):
        piece = wi[:, offs[i]:offs[i + 1]]
        cols.append(jnp.pad(piece, ((0, 0), (0, (en - st) - sizes[i]))))
    p = {'w_in': jnp.concatenate(cols, axis=1).astype(BF16)}
    p['g1'] = norm1_g[l].reshape(1, d)
    p['g2'] = norm2_g[l].reshape(1, d)
    p['dq_g'] = (jnp.tile(diff_q_g[l], 2 * DIFF_HEADS) * (DIFF_DH ** -0.5)).reshape(1, 512)
    p['dk_g'] = jnp.tile(diff_k_g[l], 2 * DIFF_HEADS).reshape(1, 512)
    p['cq_g'] = mla_cq_g[l].reshape(1, MLA_Q_LORA)
    p['ckv_g'] = mla_ckv_g[l].reshape(1, MLA_KV_LORA)
    zpad = jnp.zeros((MLA_HP - MLA_QK,), F32)
    p['mq_g'] = (jnp.tile(jnp.concatenate([mla_q_g[l], zpad]), MLA_HEADS) * (MLA_QK ** -0.5)).reshape(1, -1)
    p['mk_g'] = jnp.tile(jnp.concatenate([mla_k_g[l][:MLA_NOPE], jnp.zeros((MLA_HP - MLA_NOPE,), F32)]),
                         MLA_HEADS).reshape(1, -1)
    p['kr_g'] = jnp.pad(mla_k_g[l][MLA_NOPE:], (0, LANES - MLA_ROPE)).reshape(1, LANES)
    wq = w_uq[l].reshape(MLA_Q_LORA, MLA_HEADS, MLA_QK)
    p['w_uq'] = jnp.pad(wq, ((0, 0), (0, 0), (0, MLA_HP - MLA_QK))).reshape(MLA_Q_LORA, -1).astype(BF16)
    wkv = w_ukv[l].reshape(MLA_KV_LORA, MLA_HEADS, MLA_NOPE + MLA_V)
    p['w_uk'] = jnp.pad(wkv[:, :, :MLA_NOPE], ((0, 0), (0, 0), (0, MLA_HP - MLA_NOPE))).reshape(
        MLA_KV_LORA, -1).astype(BF16)
    wv = wkv[:, :, MLA_NOPE:].reshape(MLA_KV_LORA, MLA_HEADS // 2, 2, MLA_V)
    zv = jnp.zeros_like(wv[:, :, 0])
    wv = jnp.stack([jnp.concatenate([wv[:, :, 0], zv], axis=-1), jnp.concatenate([zv, wv[:, :, 1]], axis=-1)], axis=2)
    p['w_uv'] = wv.reshape(MLA_KV_LORA, MLA_HEADS * MLA_HP).astype(BF16)
    p['lam'] = diff_lambda[l]
    p['subln_g'] = (diff_subln_g[l] * (1.0 - lam_init)).reshape(1, DIFF_VD)
    p['lam_init'] = lam_init
    p['conv_w'] = jnp.pad(ssd_conv_w[l], ((0, SUBLANES - SSD_CONV), (0, 0)))
    p['conv_b'] = ssd_conv_b[l].reshape(1, -1)
    p['dt_bias'] = jnp.pad(ssd_dt_bias[l].reshape(-1), (0, LANES - 2 * SSD_HEADS)).reshape(1, LANES)
    p['a_log'] = jnp.pad(ssd_A_log[l].reshape(-1), (0, LANES - 2 * SSD_HEADS)).reshape(1, LANES)
    p['d_skip'] = jnp.repeat(ssd_D[l], SSD_P).reshape(1, SSD_INNER)
    p['ssd_g'] = ssd_norm_g[l].reshape(1, SSD_INNER)
    p['w_branch'] = w_branch[l].astype(BF16)
    p['w_out'] = w_out[l].astype(BF16)
    rw = jnp.pad(jnp.concatenate([moe_group_w[l], moe_expert_w[l]], axis=1),
                 ((0, 0), (0, LANES - MOE_GROUPS - MOE_EXPERTS)))
    p['rw_hi'], p['rw_lo'] = _split2(rw)
    p['rb'] = jnp.pad(jnp.concatenate([moe_group_b[l], moe_expert_b[l]]),
                      (0, LANES - MOE_GROUPS - MOE_EXPERTS)).reshape(1, LANES)
    p['w_gu'] = jnp.concatenate([moe_w_gate[l], moe_w_up[l]], axis=-1).astype(BF16)
    p['w_d'] = moe_w_down[l].astype(BF16)
    return p


def kernel(x, c, ctx, c_ctx, ada_w, ada_b, norm1_g, norm2_g, w_in, diff_q_g, diff_k_g, diff_lambda, diff_subln_g, ssd_conv_w, ssd_conv_b, ssd_dt_bias, ssd_A_log, ssd_D, ssd_norm_g, mla_cq_g, mla_ckv_g, w_uq, w_ukv, mla_q_g, mla_k_g, w_branch, w_out, moe_group_w, moe_group_b, moe_expert_w, moe_expert_b, moe_w_gate, moe_w_up, moe_w_down):
    n_batch, n_lat, d = x.shape
    n_ctx = ctx.shape[1]
    depth = w_in.shape[0]
    assert d == D_MODEL and n_ctx == TM and n_lat % TM == 0 and n_lat % GRID_W == 0
    assert n_batch + 1 <= MOD_ROWS

    cc = jnp.concatenate([c, c_ctx[None, :], jnp.zeros((MOD_ROWS - n_batch - 1, d), F32)], axis=0)
    mod = _modulation(cc, ada_w, ada_b)
    tabs = _static_tables(n_ctx, n_lat)
    xs = jnp.concatenate([ctx, x], axis=1)

    for l in range(depth):
        last = l == depth - 1
        t0 = 1 if last else 0
        p = _prep_layer(l, w_in, diff_q_g, diff_k_g, diff_lambda, diff_subln_g, ssd_conv_w, ssd_conv_b,
                        ssd_dt_bias, ssd_A_log, ssd_D, ssd_norm_g, mla_cq_g, mla_ckv_g, w_uq, w_ukv, mla_q_g,
                        mla_k_g, w_branch, w_out, norm1_g, norm2_g, moe_group_w, moe_group_b, moe_expert_w,
                        moe_expert_b, moe_w_gate, moe_w_up, moe_w_down)
        dq, dk, dv, z, xbc, dt, mq, mk, mv, gates = _in_projection(xs, mod[l], p, tabs)
        yf = _ssd_scan(xbc, dt, p, n_ctx, 0)
        ssd_o = _ssd_scan(xbc, dt, p, n_ctx, 1, yf=yf, z=z)
        diff_o = _diff_attention(dq, dk, dv, p['lam'], p['subln_g'], n_ctx, p['lam_init'], t0)
        mla_o = _mla_attention(mq, mk, mv, n_ctx, t0)
        x_mid, h2, route, counts = _merge(xs, diff_o, ssd_o, mla_o, gates, mod[l], p, t0)
        xs = _moe(x_mid, h2, route, counts, mod[l], p, t0)
    return xs
```

```python
import functools
import math

import jax
import jax.numpy as jnp
import numpy as np
from jax import lax
from jax.experimental import pallas as pl
from jax.experimental.pallas import tpu as pltpu

F32 = jnp.float32
BF16 = jnp.bfloat16
I32 = jnp.int32

D_MODEL = 1024
GRID_W = 64
ROPE_BASE = 10000.0
EPS = 1e-6
DIFF_HEADS = 4
DIFF_DH = 64
DIFF_VD = 2 * DIFF_DH
SSD_HEADS = 8
SSD_P = 64
SSD_INNER = SSD_HEADS * SSD_P
SSD_GROUPS = 2
SSD_STATE = 128
SSD_CONV = 5
SSD_CHUNK = 128
MLA_HEADS = 8
MLA_NOPE = 64
MLA_ROPE = 32
MLA_V = 64
MLA_Q_LORA = 384
MLA_KV_LORA = 256
MLA_QK = MLA_NOPE + MLA_ROPE
N_BRANCH = 3
MOE_GROUPS = 4
MOE_PER_GROUP = 8
MOE_EXPERTS = MOE_GROUPS * MOE_PER_GROUP
MOE_TOP_K = 2
MOE_HIDDEN = 256

LANES = 128
SUBLANES = 8
TM = 256
MOE_BM = 256
DIFF_HEADS_PER_STEP = 2
MLA_HEADS_PER_STEP = 4
ATTN_KEY_CHUNK = 512
VMEM_LIMIT = 56 * 1024 * 1024
HALO = SUBLANES
MOD_ROWS = 16
NEG = -1e30
LOG2E = math.log2(math.e)

SEG_DQ = (0, 512)
SEG_DK = (512, 1024)
SEG_DV = (1024, 1536)
SEG_Z = (1536, 2048)
SEG_XBC = (2048, 3072)
SEG_DT = (3072, 3200)
SEG_CQ = (3200, 3584)
SEG_CKV = (3584, 3840)
SEG_KR = (3840, 3968)
SEG_GATES = (3968, 7040)
W_IN_PAD = 7040
MLA_HP = 128

R_E1, R_E2, R_R1, R_R2, R_W1, R_W2 = 0, 1, 2, 3, 4, 5


def _cparams(sem, vmem=VMEM_LIMIT):
    return pltpu.CompilerParams(dimension_semantics=sem, vmem_limit_bytes=vmem)


def _dot(a, b):
    return jnp.dot(a, b, preferred_element_type=F32)


def _dot_nt(a, b):
    return lax.dot_general(a, b, (((1,), (1,)), ((), ())), preferred_element_type=F32)


def _split2(x):
    hi = x.astype(BF16)
    lo = (x - hi.astype(F32)).astype(BF16)
    return hi, lo


def _split3(x):
    hi = x.astype(BF16)
    r = x - hi.astype(F32)
    mid = r.astype(BF16)
    lo = (r - mid.astype(F32)).astype(BF16)
    return hi, mid, lo


def _silu(x):
    return x * jax.nn.sigmoid(x)


def _rms(x):
    return x * lax.rsqrt(jnp.mean(x * x, axis=-1, keepdims=True) + EPS)


def _group_rms_scale(x, gmat, emat):
    ms = _dot((x * x).astype(BF16), gmat)
    r = lax.rsqrt(ms + EPS)
    r_hi, r_lo = _split2(r)
    return x * (_dot(r_hi, emat) + _dot(r_lo, emat))


def _rope(x, cos, sin_a, sin_b, quarter):
    w = x.shape[-1]
    return x * cos + pltpu.roll(x, quarter, 1) * sin_a + pltpu.roll(x, w - quarter, 1) * sin_b


def _const_spec(shape):
    nd = len(shape)
    return pl.BlockSpec(shape, lambda *_: (0,) * nd, pipeline_mode=pl.Buffered(1))


def _mod_kernel(c_ref, w_ref, b_ref, o_ref):
    s = _silu(c_ref[...])
    o_ref[0] = jnp.dot(s, w_ref[0], precision=lax.Precision.HIGHEST, preferred_element_type=F32) + b_ref[0]


def _modulation(cc, ada_w, ada_b):
    depth, d, n = ada_w.shape
    tn = 1536
    return pl.pallas_call(
        _mod_kernel,
        grid=(depth, n // tn),
        in_specs=[pl.BlockSpec((MOD_ROWS, d), lambda l, j: (0, 0)),
                  pl.BlockSpec((1, d, tn), lambda l, j: (l, 0, j)),
                  pl.BlockSpec((1, 1, tn), lambda l, j: (l, 0, j))],
        out_specs=pl.BlockSpec((1, MOD_ROWS, tn), lambda l, j: (l, 0, j)),
        out_shape=jax.ShapeDtypeStruct((depth, MOD_ROWS, n), F32),
        compiler_params=_cparams(("arbitrary", "arbitrary")),
        name="adaln_mod",
    )(cc, ada_w, ada_b.reshape(depth, 1, n))


def _inproj_kernel(x_ref, mod_ref, g1_ref, w_ref, wuq_ref, wuk_ref, wuv_ref,
                   gd_ref, ed_ref, gq_ref, eq_ref, gk_ref, ek_ref, ekr_ref,
                   cosd_ref, sad_ref, sbd_ref, cosq_ref, saq_ref, sbq_ref, cosk_ref, sak_ref, sbk_ref,
                   dqg_ref, dkg_ref, cqg_ref, ckvg_ref, mqg_ref, mkg_ref, krg_ref, mvo_ref,
                   dq_ref, dk_ref, dv_ref, z_ref, xbc_ref, dt_ref, mq_ref, mk_ref, mv_ref, gates_ref,
                   *, n_batch):
    t = pl.program_id(0)
    b = pl.program_id(1)
    row = jnp.where(t == 0, n_batch, b)
    shift = mod_ref[pl.ds(row, 1), 0:D_MODEL]
    scale = mod_ref[pl.ds(row, 1), D_MODEL:2 * D_MODEL]
    xn = _rms(x_ref[0]) * g1_ref[...]
    h = (xn * (1.0 + scale) + shift).astype(BF16)

    big = _dot(h, w_ref[...])

    def proj(seg):
        return big[:, seg[0]:seg[1]]

    cosd, sad, sbd = cosd_ref[...], sad_ref[...], sbd_ref[...]
    dq = _group_rms_scale(proj(SEG_DQ), gd_ref[...], ed_ref[...]) * dqg_ref[...]
    dq_ref[0] = _rope(dq, cosd, sad, sbd, DIFF_DH // 4).astype(BF16)
    dk = _group_rms_scale(proj(SEG_DK), gd_ref[...], ed_ref[...]) * dkg_ref[...]
    dk_ref[0] = _rope(dk, cosd, sad, sbd, DIFF_DH // 4).astype(BF16)
    dv_ref[0] = proj(SEG_DV).astype(BF16)
    z_ref[0] = proj(SEG_Z).astype(BF16)
    xbc_ref[0] = proj(SEG_XBC)
    dt_ref[0] = proj(SEG_DT)
    gates_ref[0] = jax.nn.sigmoid(proj(SEG_GATES)).astype(BF16)

    cq = _rms(proj(SEG_CQ)) * cqg_ref[...]
    q = _dot(cq.astype(BF16), wuq_ref[...])
    q = _group_rms_scale(q, gq_ref[...], eq_ref[...]) * mqg_ref[...]
    mq_ref[0] = _rope(q, cosq_ref[...], saq_ref[...], sbq_ref[...], MLA_ROPE // 4).astype(BF16)

    ckv = (_rms(proj(SEG_CKV)) * ckvg_ref[...]).astype(BF16)
    kn = _dot(ckv, wuk_ref[...])
    kn = _group_rms_scale(kn, gk_ref[...], ek_ref[...]) * mkg_ref[...]
    mv_ref[0] = (_dot(ckv, wuv_ref[...]) + mvo_ref[...]).astype(BF16)
    kr = proj(SEG_KR)
    kr = kr * lax.rsqrt(jnp.sum(kr * kr, axis=-1, keepdims=True) * (1.0 / MLA_ROPE) + EPS) * krg_ref[...]
    kr = _rope(kr, cosk_ref[...], sak_ref[...], sbk_ref[...], MLA_ROPE // 4).astype(BF16)
    mk_ref[0] = (kn + _dot(kr, ekr_ref[...])).astype(BF16)


def _rope_tables(n_ctx, n_lat, dim, lane_layout):
    quarter = dim // 4
    inv_freq = ROPE_BASE ** (-jnp.arange(quarter, dtype=F32) / quarter)
    s = jnp.arange(n_lat, dtype=I32)
    ar = (s // GRID_W).astype(F32)[:, None] * inv_freq
    ac = (s % GRID_W).astype(F32)[:, None] * inv_freq
    ang = jnp.concatenate([ar, ar, ac, ac], axis=-1)
    ang = jnp.concatenate([jnp.zeros((n_ctx, dim), F32), ang], axis=0)
    odd = jnp.asarray((np.arange(dim) // quarter) % 2 == 1)
    cos_d, sin_d = jnp.cos(ang), jnp.sin(ang)
    units = (cos_d, jnp.where(odd, sin_d, 0.0), jnp.where(odd, 0.0, -sin_d))
    out = []
    for unit, fill in zip(units, (1.0, 0.0, 0.0)):
        pieces = [unit if item == 'rope' else jnp.full((n_ctx + n_lat, item), fill, F32) for item in lane_layout]
        out.append(jnp.concatenate(pieces, axis=-1))
    return out


def _group_mats(width, groups):
    g = np.zeros((width, LANES), np.float32)
    e = np.zeros((LANES, width), np.float32)
    for i, (st, sz) in enumerate(groups):
        g[st:st + sz, i] = 1.0 / sz
        e[i, st:st + sz] = 1.0
    return jnp.asarray(g, BF16), jnp.asarray(e, BF16)


def _static_tables(n_ctx, n_lat):
    tabs = {}
    tabs['gd'], tabs['ed'] = _group_mats(512, [(i * DIFF_DH, DIFF_DH) for i in range(2 * DIFF_HEADS)])
    qgroups = []
    for h in range(MLA_HEADS):
        qgroups += [(h * MLA_HP, MLA_NOPE), (h * MLA_HP + MLA_NOPE, MLA_ROPE)]
    tabs['gq'], tabs['eq'] = _group_mats(MLA_HEADS * MLA_HP, qgroups)
    tabs['gk'], tabs['ek'] = _group_mats(MLA_HEADS * MLA_HP, [(h * MLA_HP, MLA_NOPE) for h in range(MLA_HEADS)])
    ekr = np.zeros((LANES, MLA_HEADS * MLA_HP), np.float32)
    for h in range(MLA_HEADS):
        for j in range(MLA_ROPE):
            ekr[j, h * MLA_HP + MLA_NOPE + j] = 1.0
    tabs['ekr'] = jnp.asarray(ekr, BF16)
    mv_ones = np.zeros((1, MLA_HEADS * MLA_HP), np.float32)
    for h in range(MLA_HEADS):
        mv_ones[0, h * MLA_HP + (MLA_V if h % 2 == 0 else 0)] = 1.0
    tabs['mv_ones'] = jnp.asarray(mv_ones)
    tabs['cosd'], tabs['sad'], tabs['sbd'] = _rope_tables(n_ctx, n_lat, DIFF_DH, ['rope'] * (2 * DIFF_HEADS))
    tabs['cosq'], tabs['saq'], tabs['sbq'] = _rope_tables(
        n_ctx, n_lat, MLA_ROPE, [MLA_NOPE, 'rope', MLA_HP - MLA_QK] * MLA_HEADS)
    tabs['cosk'], tabs['sak'], tabs['sbk'] = _rope_tables(n_ctx, n_lat, MLA_ROPE, ['rope', LANES - MLA_ROPE])
    return tabs


def _in_projection(xs, mod_l, p, tabs):
    n_batch, n_rows, d = xs.shape
    nt = n_rows // TM
    row_spec = lambda w: pl.BlockSpec((1, TM, w), lambda t, b: (b, t, 0))
    tab_spec = lambda w: pl.BlockSpec((TM, w), lambda t, b: (t, 0))
    consts = [mod_l, p['g1'], p['w_in'], p['w_uq'], p['w_uk'], p['w_uv'],
              tabs['gd'], tabs['ed'], tabs['gq'], tabs['eq'], tabs['gk'], tabs['ek'], tabs['ekr']]
    rope = [tabs['cosd'], tabs['sad'], tabs['sbd'], tabs['cosq'], tabs['saq'], tabs['sbq'],
            tabs['cosk'], tabs['sak'], tabs['sbk']]
    gains = [p['dq_g'], p['dk_g'], p['cq_g'], p['ckv_g'], p['mq_g'], p['mk_g'], p['kr_g'], tabs['mv_ones']]
    out_w = [(512, BF16), (512, BF16), (512, BF16), (512, BF16), (1024, F32), (LANES, F32),
             (MLA_HEADS * MLA_HP, BF16), (MLA_HEADS * MLA_HP, BF16), (MLA_HEADS * MLA_HP, BF16),
             (N_BRANCH * D_MODEL, BF16)]
    return pl.pallas_call(
        functools.partial(_inproj_kernel, n_batch=n_batch),
        grid=(nt, n_batch),
        in_specs=([row_spec(d)] + [_const_spec(a.shape) for a in consts]
                  + [tab_spec(a.shape[1]) for a in rope] + [_const_spec(a.shape) for a in gains]),
        out_specs=[row_spec(w) for w, _ in out_w],
        out_shape=[jax.ShapeDtypeStruct((n_batch, n_rows, w), dt) for w, dt in out_w],
        compiler_params=_cparams(("arbitrary", "arbitrary")),
        name="in_proj",
    )(xs, *consts, *rope, *gains)


def _ssd_chunk_of_step(s, direction, n_ctx_chunks, n_chunks):
    if direction == 0:
        return s
    return jnp.where(s < n_ctx_chunks, n_ctx_chunks - 1 - s, n_chunks - 1 + n_ctx_chunks - s)


def _ssd_kernel(*refs, direction, n_ctx_chunks, n_chunks):
    if direction == 0:
        (xbc_ref, prev_ref, next_ref, dt_ref, cw_ref, cb_ref, dtb_ref, alog_ref, y_ref, u_ref, state_ref) = refs
    else:
        (u_ref, dt_ref, dtb_ref, alog_ref, yf_ref, z_ref, dskip_ref, ng_ref, y_ref, state_ref) = refs
    s = pl.program_id(1)
    c = _ssd_chunk_of_step(s, direction, n_ctx_chunks, n_chunks)
    ck = SSD_CHUNK

    @pl.when(s == 0)
    def _():
        state_ref[...] = jnp.zeros_like(state_ref)

    if direction == 0:
        first = jnp.logical_or(c == 0, c == n_ctx_chunks)
        last = jnp.logical_or(c == n_ctx_chunks - 1, c == n_chunks - 1)
        prev = jnp.where(first, 0.0, prev_ref[0])
        nxt = jnp.where(last, 0.0, next_ref[0])
        full = jnp.concatenate([prev, xbc_ref[0], nxt], axis=0)
        n_full = ck + 2 * HALO
        half = (SSD_CONV - 1) // 2
        conv = None
        for k in range(SSD_CONV):
            sh = (half - k) % n_full
            rolled = full if sh == 0 else pltpu.roll(full, sh, 0)
            term = rolled[HALO:HALO + ck] * cw_ref[k:k + 1, :]
            conv = term if conv is None else conv + term
        u = _silu(conv + cb_ref[...])
        u_ref[0] = u.astype(BF16)
    else:
        u = u_ref[0].astype(F32)
    xs = u[:, :SSD_INNER]
    bm = u[:, SSD_INNER:SSD_INNER + SSD_GROUPS * SSD_STATE]
    cm = u[:, SSD_INNER + SSD_GROUPS * SSD_STATE:]

    dt = jax.nn.softplus(dt_ref[0] + dtb_ref[...])
    a = dt * (-jnp.exp(alog_ref[...]))
    ri = lax.broadcasted_iota(I32, (ck, ck), 0)
    ci = lax.broadcasted_iota(I32, (ck, ck), 1)
    valid = (ci <= ri) if direction == 0 else (ci >= ri)
    tri = jnp.where(valid, 1.0, 0.0).astype(BF16)
    a1, a2, a3 = _split3(a)
    acum = _dot(tri, a1) + _dot(tri, a2) + _dot(tri, a3)
    end_row = ck - 1 if direction == 0 else 0
    atot = acum[end_row:end_row + 1, :]
    w_end = jnp.exp(atot - acum) * dt
    e_in = jnp.exp(acum)
    cdec = jnp.exp(atot)
    acum_t = acum.T
    dt_t = dt.T
    lane = lax.broadcasted_iota(I32, (1, LANES), 1)
    left = lane < SSD_P

    pairs_per_group = SSD_HEADS // SSD_GROUPS // 2
    ys = []
    for g in range(SSD_GROUPS):
        bg = bm[:, g * SSD_STATE:(g + 1) * SSD_STATE]
        cg = cm[:, g * SSD_STATE:(g + 1) * SSD_STATE].astype(BF16)
        cb = _dot_nt(cg, bg.astype(BF16))
        bg_t = bg.T.astype(BF16)
        for pp in range(pairs_per_group):
            pr = g * pairs_per_group + pp
            xs_p = xs[:, pr * LANES:(pr + 1) * LANES]
            c0 = direction * SSD_HEADS + 2 * pr
            yd = None
            for side in range(2):
                col = c0 + side
                seg = acum[:, col:col + 1] - acum_t[col:col + 1, :]
                decay = jnp.where(valid, jnp.exp(jnp.where(valid, seg, 0.0)), 0.0)
                mix = (cb * decay * dt_t[col:col + 1, :]).astype(BF16)
                keep = left if side == 0 else jnp.logical_not(left)
                part = _dot(mix, jnp.where(keep, xs_p, 0.0).astype(BF16))
                yd = part if yd is None else yd + part
            pick = lambda m: jnp.where(left, m[:, c0:c0 + 1], m[:, c0 + 1:c0 + 2])
            h_in = state_ref[pr]
            y_off = _dot(cg, h_in.astype(BF16)) * pick(e_in)
            xw = (xs_p * pick(w_end)).astype(BF16)
            state_ref[pr] = h_in * pick(cdec) + _dot(bg_t, xw)
            ys.append(yd + y_off)
    y = jnp.concatenate(ys, axis=-1)
    if direction == 0:
        y_ref[0] = y
    else:
        y = y + yf_ref[0] + dskip_ref[...] * xs
        y = y * _silu(z_ref[0].astype(F32))
        y_ref[0] = (_rms(y) * ng_ref[...]).astype(BF16)


def _ssd_scan(xbc, dt, p, n_ctx, direction, yf=None, z=None):
    n_batch, n_rows, wc = xbc.shape
    ck = SSD_CHUNK
    n_chunks = n_rows // ck
    n_ctx_chunks = n_ctx // ck
    n_halo_blocks = n_rows // HALO
    per = ck // HALO
    cmap = lambda s: _ssd_chunk_of_step(s, direction, n_ctx_chunks, n_chunks)
    cur = lambda w: pl.BlockSpec((1, ck, w), lambda b, s: (b, cmap(s), 0))
    if direction == 0:
        in_specs = [cur(wc),
                    pl.BlockSpec((1, HALO, wc), lambda b, s: (b, jnp.maximum(cmap(s) * per - 1, 0), 0)),
                    pl.BlockSpec((1, HALO, wc),
                                 lambda b, s: (b, jnp.minimum((cmap(s) + 1) * per, n_halo_blocks - 1), 0)),
                    cur(LANES),
                    _const_spec(p['conv_w'].shape), _const_spec(p['conv_b'].shape),
                    _const_spec(p['dt_bias'].shape), _const_spec(p['a_log'].shape)]
        args = [xbc, xbc, xbc, dt, p['conv_w'], p['conv_b'], p['dt_bias'], p['a_log']]
        out_specs = [cur(SSD_INNER), cur(wc)]
        out_shape = [jax.ShapeDtypeStruct((n_batch, n_rows, SSD_INNER), F32),
                     jax.ShapeDtypeStruct((n_batch, n_rows, wc), BF16)]
    else:
        in_specs = [cur(wc), cur(LANES), _const_spec(p['dt_bias'].shape), _const_spec(p['a_log'].shape),
                    cur(SSD_INNER), cur(SSD_INNER), _const_spec(p['d_skip'].shape), _const_spec(p['ssd_g'].shape)]
        args = [xbc, dt, p['dt_bias'], p['a_log'], yf, z, p['d_skip'], p['ssd_g']]
        out_specs = cur(SSD_INNER)
        out_shape = jax.ShapeDtypeStruct((n_batch, n_rows, SSD_INNER), BF16)
    return pl.pallas_call(
        functools.partial(_ssd_kernel, direction=direction, n_ctx_chunks=n_ctx_chunks, n_chunks=n_chunks),
        grid=(n_batch, n_chunks),
        in_specs=in_specs,
        out_specs=out_specs,
        out_shape=out_shape,
        scratch_shapes=[pltpu.VMEM((SSD_HEADS // 2, SSD_STATE, LANES), F32)],
        compiler_params=_cparams(("arbitrary", "arbitrary")),
        name="ssd_fwd" if direction == 0 else "ssd_bwd",
    )(*args)


def _softmax_parts(s2):
    m = jnp.max(s2, axis=-1, keepdims=True)
    e = jnp.exp2(s2 - m)
    return e, jnp.sum(e, axis=-1, keepdims=True)


def _softmax_numerator(s2):
    m = jnp.max(s2, axis=-1, keepdims=True)
    return jnp.exp2((s2 - m).astype(BF16))


def _diff_attn_kernel(q_ref, k_ref, v_ref, lam_ref, sg_ref, o_ref, *, n_ctx, lam_init, t0):
    lv = lam_ref[...]
    lam = (jnp.exp(jnp.sum(lv[0:1] * lv[1:2], axis=-1, keepdims=True))
           - jnp.exp(jnp.sum(lv[2:3] * lv[3:4], axis=-1, keepdims=True)) + lam_init)
    lane = lax.broadcasted_iota(I32, (1, DIFF_VD), 1)

    def attend(n_keys):
        for hd in range(DIFF_HEADS_PER_STEP):
            sl = slice(hd * DIFF_VD, (hd + 1) * DIFF_VD)
            q = q_ref[0, :, sl]
            q0 = jnp.where(lane < DIFF_DH, q, jnp.zeros_like(q))
            q1 = jnp.where(lane < DIFF_DH, jnp.zeros_like(q), q)
            k = k_ref[0, 0:n_keys, sl]
            e0, l0 = _softmax_parts(_dot_nt(q0, k))
            e1, l1 = _softmax_parts(_dot_nt(q1, k))
            w = (e0 - (lam * l0 / l1) * e1).astype(BF16)
            o = _dot(w, v_ref[0, 0:n_keys, sl]) * (1.0 / l0)
            o_ref[0, :, sl] = (_rms(o) * sg_ref[...]).astype(BF16)

    is_ctx = t0 + pl.program_id(2) == 0

    @pl.when(is_ctx)
    def _():
        attend(n_ctx)

    @pl.when(jnp.logical_not(is_ctx))
    def _():
        attend(k_ref.shape[1])


def _diff_attention(dq, dk, dv, lam_p, subln_g, n_ctx, lam_init, t0):
    n_batch, n_rows, _ = dq.shape
    nq = n_rows // TM - t0
    return pl.pallas_call(
        functools.partial(_diff_attn_kernel, n_ctx=n_ctx, lam_init=lam_init, t0=t0),
        grid=(n_batch, DIFF_HEADS // DIFF_HEADS_PER_STEP, nq),
        in_specs=[pl.BlockSpec((1, TM, DIFF_HEADS_PER_STEP * DIFF_VD), lambda b, h, t: (b, t0 + t, h)),
                  pl.BlockSpec((1, n_rows, DIFF_HEADS_PER_STEP * DIFF_VD), lambda b, h, t: (b, 0, h)),
                  pl.BlockSpec((1, n_rows, DIFF_HEADS_PER_STEP * DIFF_VD), lambda b, h, t: (b, 0, h)),
                  _const_spec(lam_p.shape), _const_spec(subln_g.shape)],
        out_specs=pl.BlockSpec((1, TM, DIFF_HEADS_PER_STEP * DIFF_VD), lambda b, h, t: (b, t, h)),
        out_shape=jax.ShapeDtypeStruct((n_batch, nq * TM, DIFF_HEADS * DIFF_VD), BF16),
        compiler_params=_cparams(("arbitrary", "arbitrary", "arbitrary")),
        name="diff_attn",
    )(dq, dk, dv, lam_p, subln_g)


def _mla_attn_kernel(q_ref, k_ref, v_ref, o_ref, *, n_ctx, t0):
    lane = lax.broadcasted_iota(I32, (1, MLA_HP), 1)

    def attend(n_keys):
        res = []
        for hd in range(MLA_HEADS_PER_STEP):
            sl = slice(hd * MLA_HP, (hd + 1) * MLA_HP)
            e = _softmax_numerator(_dot_nt(q_ref[0, :, sl], k_ref[0, 0:n_keys, sl]))
            res.append(_dot(e, v_ref[0, 0:n_keys, sl]))
        for pr in range(MLA_HEADS_PER_STEP // 2):
            r0, r1 = res[2 * pr], res[2 * pr + 1]
            o0 = r0 * (1.0 / r0[:, MLA_V:MLA_V + 1])
            o1 = r1 * (1.0 / r1[:, 0:1])
            o_ref[0, :, pr * MLA_HP:(pr + 1) * MLA_HP] = jnp.where(lane < MLA_V, o0, o1).astype(BF16)

    is_ctx = t0 + pl.program_id(2) == 0

    @pl.when(is_ctx)
    def _():
        attend(n_ctx)

    @pl.when(jnp.logical_not(is_ctx))
    def _():
        attend(k_ref.shape[1])


def _mla_attention(mq, mk, mv, n_ctx, t0):
    n_batch, n_rows, _ = mq.shape
    nq = n_rows // TM - t0
    return pl.pallas_call(
        functools.partial(_mla_attn_kernel, n_ctx=n_ctx, t0=t0),
        grid=(n_batch, MLA_HEADS // MLA_HEADS_PER_STEP, nq),
        in_specs=[pl.BlockSpec((1, TM, MLA_HEADS_PER_STEP * MLA_HP), lambda b, h, t: (b, t0 + t, h)),
                  pl.BlockSpec((1, n_rows, MLA_HEADS_PER_STEP * MLA_HP), lambda b, h, t: (b, 0, h)),
                  pl.BlockSpec((1, n_rows, MLA_HEADS_PER_STEP * MLA_HP), lambda b, h, t: (b, 0, h))],
        out_specs=pl.BlockSpec((1, TM, MLA_HEADS_PER_STEP * MLA_V), lambda b, h, t: (b, t, h)),
        out_shape=jax.ShapeDtypeStruct((n_batch, nq * TM, MLA_HEADS * MLA_V), BF16),
        compiler_params=_cparams(("arbitrary", "arbitrary", "arbitrary")),
        name="mla_attn",
    )(mq, mk, mv)


def _merge_kernel(x_ref, da_ref, ss_ref, ma_ref, gates_ref, mod_ref, wb_ref, wo_ref, g2_ref,
                  rwh_ref, rwl_ref, rb_ref,
                  xo_ref, h2_ref, route_ref, cnt_ref, base_ref, *, n_batch, t0):
    b = pl.program_id(0)
    t = pl.program_id(1)

    @pl.when(jnp.logical_and(b == 0, t == 0))
    def _():
        base_ref[...] = jnp.zeros_like(base_ref)

    row = jnp.where(t0 + t == 0, n_batch, b)
    gate1 = mod_ref[pl.ds(row, 1), 2 * D_MODEL:3 * D_MODEL]
    shift2 = mod_ref[pl.ds(row, 1), 3 * D_MODEL:4 * D_MODEL]
    scale2 = mod_ref[pl.ds(row, 1), 4 * D_MODEL:5 * D_MODEL]

    acc = None
    for k, o_ref in enumerate((da_ref, ss_ref, ma_ref)):
        term = gates_ref[0, :, k * D_MODEL:(k + 1) * D_MODEL].astype(F32) * _dot(o_ref[0], wb_ref[k])
        acc = term if acc is None else acc + term
    x = x_ref[0] + gate1 * _dot(acc.astype(BF16), wo_ref[...])
    xo_ref[0] = x
    h2 = _rms(x) * g2_ref[...] * (1.0 + scale2) + shift2
    h2_ref[0] = h2

    h_hi, h_lo = _split2(h2)
    lg = _dot(h_hi, rwh_ref[...]) + _dot(h_lo, rwh_ref[...]) + _dot(h_hi, rwl_ref[...]) + rb_ref[...]
    lane = lax.broadcasted_iota(I32, lg.shape, 1)
    lane_f = lane.astype(F32)
    big = float(LANES)
    is_g = lane < MOE_GROUPS
    gl = jnp.where(is_g, lg, NEG)
    gmax = jnp.max(gl, axis=-1, keepdims=True)
    gidx = jnp.min(jnp.where(gl == gmax, lane_f, big), axis=-1, keepdims=True)
    pg = 1.0 / jnp.sum(jnp.where(is_g, jnp.exp(gl - gmax), 0.0), axis=-1, keepdims=True)
    lo_lane = MOE_GROUPS + gidx * MOE_PER_GROUP
    in_grp = jnp.logical_and(lane_f >= lo_lane, lane_f < lo_lane + MOE_PER_GROUP)
    el = jnp.where(in_grp, lg, NEG)
    m1 = jnp.max(el, axis=-1, keepdims=True)
    i1 = jnp.min(jnp.where(el == m1, lane_f, big), axis=-1, keepdims=True)
    el2 = jnp.where(lane_f == i1, NEG, el)
    m2 = jnp.max(el2, axis=-1, keepdims=True)
    i2 = jnp.min(jnp.where(el2 == m2, lane_f, big), axis=-1, keepdims=True)
    tt = jnp.exp(m2 - m1)
    w1 = pg / (1.0 + tt)
    w2 = pg * tt / (1.0 + tt)
    e1 = i1 - MOE_GROUPS
    e2 = i2 - MOE_GROUPS

    oh1 = lane_f == e1
    oh2 = lane_f == e2
    oh = jnp.where(jnp.logical_or(oh1, oh2), 1.0, 0.0)
    tm = oh.shape[0]
    ri = lax.broadcasted_iota(I32, (tm, tm), 0)
    ci = lax.broadcasted_iota(I32, (tm, tm), 1)
    strict = jnp.where(ci < ri, 1.0, 0.0).astype(BF16)
    base = base_ref[0:1, :]
    rank_all = _dot(strict, oh.astype(BF16)) + base
    r1 = jnp.sum(jnp.where(oh1, rank_all, 0.0), axis=-1, keepdims=True)
    r2 = jnp.sum(jnp.where(oh2, rank_all, 0.0), axis=-1, keepdims=True)
    new_base = base + jnp.sum(oh, axis=0, keepdims=True)
    base_ref[...] = jnp.broadcast_to(new_base, base_ref.shape)
    cnt_ref[...] = jnp.broadcast_to(new_base, cnt_ref.shape)

    rec = jnp.zeros(lg.shape, F32)
    for ln, val in ((R_E1, e1), (R_E2, e2), (R_R1, r1), (R_R2, r2), (R_W1, w1), (R_W2, w2)):
        rec = jnp.where(lane == ln, val, rec)
    route_ref[0] = rec


def _merge(xs, da, ss, ma, gates, mod_l, p, t0):
    n_batch, n_rows, d = xs.shape
    nt = n_rows // TM - t0
    rows_out = nt * TM
    in_row = lambda w: pl.BlockSpec((1, TM, w), lambda b, t: (b, t0 + t, 0))
    out_row = lambda w: pl.BlockSpec((1, TM, w), lambda b, t: (b, t, 0))
    consts = [mod_l, p['w_branch'], p['w_out'], p['g2'], p['rw_hi'], p['rw_lo'], p['rb']]
    return pl.pallas_call(
        functools.partial(_merge_kernel, n_batch=n_batch, t0=t0),
        grid=(n_batch, nt),
        in_specs=[in_row(d), out_row(512), in_row(512), out_row(512), in_row(N_BRANCH * D_MODEL)]
                 + [_const_spec(a.shape) for a in consts],
        out_specs=[out_row(d), out_row(d), out_row(LANES), pl.BlockSpec((SUBLANES, LANES), lambda b, t: (0, 0))],
        out_shape=[jax.ShapeDtypeStruct((n_batch, rows_out, d), F32),
                   jax.ShapeDtypeStruct((n_batch, rows_out, d), F32),
                   jax.ShapeDtypeStruct((n_batch, rows_out, LANES), F32),
                   jax.ShapeDtypeStruct((SUBLANES, LANES), F32)],
        scratch_shapes=[pltpu.VMEM((SUBLANES, LANES), F32)],
        compiler_params=_cparams(("arbitrary", "arbitrary")),
        name="merge_route",
    )(xs, da, ss, ma, gates, *consts)


def _row_copy(src_ref, src_row, dst_ref, dst_row, sem):
    return pltpu.make_async_copy(src_ref.at[pl.ds(src_row, 1), :], dst_ref.at[pl.ds(dst_row, 1), :], sem)


def _slot(seg_ref, ids_ref, which, r, tm):
    return seg_ref[ids_ref[0, 0, which * tm + r]] + ids_ref[0, 0, (MOE_TOP_K + which) * tm + r]


def _dispatch_kernel(seg_ref, ids_ref, h2_ref, xs_ref, zero_ref, sem, zsem):
    tm = h2_ref.shape[1]
    src = h2_ref.at[0]

    @pl.when(jnp.logical_and(pl.program_id(0) == 0, pl.program_id(1) == 0))
    def _():
        zero_ref[...] = jnp.zeros_like(zero_ref)

        def zero_tail(e, carry):
            start = seg_ref[e]
            end = seg_ref[e + 1]

            @pl.when(end > start)
            def _():
                cp = pltpu.make_async_copy(
                    zero_ref, xs_ref.at[pl.ds(pl.multiple_of(end - MOE_BM, MOE_BM), MOE_BM), :], zsem)
                cp.start()
                cp.wait()
            return carry

        lax.fori_loop(0, MOE_EXPERTS, zero_tail, 0)

        def zero_unused(j, carry):
            cp = pltpu.make_async_copy(zero_ref, xs_ref.at[pl.ds(pl.multiple_of(j * MOE_BM, MOE_BM), MOE_BM), :], zsem)
            cp.start()
            cp.wait()
            return carry

        lax.fori_loop(seg_ref[MOE_EXPERTS] // MOE_BM, xs_ref.shape[0] // MOE_BM, zero_unused, 0)

    def issue(r, carry):
        for which in range(MOE_TOP_K):
            _row_copy(src, r, xs_ref, _slot(seg_ref, ids_ref, which, r, tm), sem).start(priority=which)
        return carry

    lax.fori_loop(0, tm, issue, 0, unroll=8)
    for _ in range(MOE_TOP_K):
        pltpu.make_async_copy(src, xs_ref.at[pl.ds(0, tm), :], sem).wait()


def _dispatch(h2, seg_start, ids, n_slots):
    n_batch, n_rows, d = h2.shape
    nt = n_rows // TM
    grid_spec = pltpu.PrefetchScalarGridSpec(
        num_scalar_prefetch=1,
        grid=(n_batch, nt),
        in_specs=[pl.BlockSpec((1, 1, 2 * MOE_TOP_K * TM), lambda b, t, seg: (b * nt + t, 0, 0),
                               memory_space=pltpu.SMEM),
                  pl.BlockSpec((1, TM, d), lambda b, t, seg: (b, t, 0))],
        out_specs=pl.BlockSpec(memory_space=pl.ANY),
        scratch_shapes=[pltpu.VMEM((MOE_BM, d), F32), pltpu.SemaphoreType.DMA(()), pltpu.SemaphoreType.DMA(())],
    )
    return pl.pallas_call(
        _dispatch_kernel,
        grid_spec=grid_spec,
        out_shape=jax.ShapeDtypeStruct((n_slots, d), F32),
        compiler_params=_cparams(("arbitrary", "arbitrary")),
        name="moe_dispatch",
    )(seg_start, ids, h2)


def _expert_kernel(be_ref, nu_ref, xs_ref, wgu_ref, wd_ref, y_ref):
    i = pl.program_id(0)

    @pl.when(i < nu_ref[0])
    def _():
        gu = _dot(xs_ref[...].astype(BF16), wgu_ref[0])
        hid = _silu(gu[:, :MOE_HIDDEN]) * gu[:, MOE_HIDDEN:]
        y_ref[...] = _dot(hid.astype(BF16), wd_ref[0])

    @pl.when(i >= nu_ref[0])
    def _():
        y_ref[...] = jnp.zeros_like(y_ref)


def _expert_ffn(xs, blk_expert, n_used, w_gu, w_d):
    n_slots, d = xs.shape
    n_blocks = n_slots // MOE_BM
    grid_spec = pltpu.PrefetchScalarGridSpec(
        num_scalar_prefetch=2,
        grid=(n_blocks,),
        in_specs=[pl.BlockSpec((MOE_BM, d), lambda i, be, nu: (jnp.minimum(i, nu[0] - 1), 0)),
                  pl.BlockSpec((1, d, 2 * MOE_HIDDEN), lambda i, be, nu: (be[i], 0, 0)),
                  pl.BlockSpec((1, MOE_HIDDEN, d), lambda i, be, nu: (be[i], 0, 0))],
        out_specs=pl.BlockSpec((MOE_BM, d), lambda i, be, nu: (i, 0)),
    )
    return pl.pallas_call(
        _expert_kernel,
        grid_spec=grid_spec,
        out_shape=jax.ShapeDtypeStruct((n_slots, d), F32),
        compiler_params=_cparams(("arbitrary",)),
        name="moe_experts",
    )(blk_expert, n_used, xs, w_gu, w_d)


def _combine_kernel(seg_ref, ids_ref, x_ref, route_ref, mod_ref, y_ref, o_ref, buf_ref, sem, *, n_batch, t0):
    b = pl.program_id(0)
    t = pl.program_id(1)
    tm = x_ref.shape[1]

    def issue(r, carry):
        for which in range(MOE_TOP_K):
            _row_copy(y_ref, _slot(seg_ref, ids_ref, which, r, tm), buf_ref.at[which], r, sem).start(priority=which)
        return carry

    lax.fori_loop(0, tm, issue, 0, unroll=8)
    for which in range(MOE_TOP_K):
        pltpu.make_async_copy(y_ref.at[pl.ds(0, tm), :], buf_ref.at[which], sem).wait()

    row = jnp.where(t0 + t == 0, n_batch, b)
    gate2 = mod_ref[pl.ds(row, 1), 5 * D_MODEL:6 * D_MODEL]
    rec = route_ref[0]
    f = rec[:, R_W1:R_W1 + 1] * buf_ref[0] + rec[:, R_W2:R_W2 + 1] * buf_ref[1]
    o_ref[0] = x_ref[0] + gate2 * f


def _combine(x_mid, route, seg_start, ids, y, mod_l, t0):
    n_batch, n_rows, d = x_mid.shape
    nt = n_rows // TM
    row = lambda w: pl.BlockSpec((1, TM, w), lambda b, t, seg: (b, t, 0))
    grid_spec = pltpu.PrefetchScalarGridSpec(
        num_scalar_prefetch=1,
        grid=(n_batch, nt),
        in_specs=[pl.BlockSpec((1, 1, 2 * MOE_TOP_K * TM), lambda b, t, seg: (b * nt + t, 0, 0),
                               memory_space=pltpu.SMEM),
                  row(d), row(LANES),
                  pl.BlockSpec(mod_l.shape, lambda b, t, seg: (0, 0), pipeline_mode=pl.Buffered(1)),
                  pl.BlockSpec(memory_space=pl.ANY)],
        out_specs=row(d),
        scratch_shapes=[pltpu.VMEM((MOE_TOP_K, TM, d), F32), pltpu.SemaphoreType.DMA(())],
    )
    return pl.pallas_call(
        functools.partial(_combine_kernel, n_batch=n_batch, t0=t0),
        grid_spec=grid_spec,
        out_shape=jax.ShapeDtypeStruct((n_batch, n_rows, d), F32),
        compiler_params=_cparams(("arbitrary", "arbitrary")),
        name="moe_combine",
    )(seg_start, ids, x_mid, route, mod_l, y)


def _moe(x_mid, h2, route, counts, mod_l, p, t0):
    n_batch, n_rows, d = h2.shape
    n_tok = n_batch * n_rows
    n_blocks = -(-(n_tok * MOE_TOP_K) // MOE_BM) + MOE_EXPERTS
    cnt = counts[0, :MOE_EXPERTS].astype(I32)
    padded = (cnt + MOE_BM - 1) // MOE_BM * MOE_BM
    pad_end = jnp.cumsum(padded)
    seg_start = jnp.concatenate([jnp.zeros((1,), I32), pad_end]).astype(I32)
    blk_first = jnp.arange(n_blocks, dtype=I32) * MOE_BM
    blk_expert = jnp.minimum(jnp.sum((pad_end[None, :] <= blk_first[:, None]).astype(I32), axis=1), MOE_EXPERTS - 1)
    n_used = (pad_end[-1:] // MOE_BM).astype(I32)
    ids = route[..., R_E1:R_R2 + 1].astype(I32).reshape(n_tok // TM, TM, 2 * MOE_TOP_K)
    ids = jnp.swapaxes(ids, 1, 2).reshape(n_tok // TM, 1, 2 * MOE_TOP_K * TM)
    xs = _dispatch(h2, seg_start, ids, n_blocks * MOE_BM)
    y = _expert_ffn(xs, blk_expert, n_used, p['w_gu'], p['w_d'])
    return _combine(x_mid, route, seg_start, ids, y, mod_l, t0)


def _prep_layer(l, w_in, diff_q_g, diff_k_g, diff_lambda, diff_subln_g, ssd_conv_w, ssd_conv_b, ssd_dt_bias,
                ssd_A_log, ssd_D, ssd_norm_g, mla_cq_g, mla_ckv_g, w_uq, w_ukv, mla_q_g, mla_k_g, w_branch, w_out,
                norm1_g, norm2_g, moe_group_w, moe_group_b, moe_expert_w, moe_expert_b, moe_w_gate, moe_w_up,
                moe_w_down):
    d = D_MODEL
    lam_init = 0.8 - 0.6 * math.exp(-0.3 * l)
    wi = w_in[l]
    sizes = (512, 512, 512, 512, 1024, 16, MLA_Q_LORA, MLA_KV_LORA, MLA_ROPE, N_BRANCH * d)
    offs = np.concatenate([[0], np.cumsum(sizes)])
    segs = (SEG_DQ, SEG_DK, SEG_DV, SEG_Z, SEG_XBC, SEG_DT, SEG_CQ, SEG_CKV, SEG_KR, SEG_GATES)
    cols = []
    for i, (st, en) in enumerate(segs):
        piece = wi[:, offs[i]:offs[i + 1]]
        cols.append(jnp.pad(piece, ((0, 0), (0, (en - st) - sizes[i]))))
    p = {'w_in': jnp.concatenate(cols, axis=1).astype(BF16)}
    p['g1'] = norm1_g[l].reshape(1, d)
    p['g2'] = norm2_g[l].reshape(1, d)
    p['dq_g'] = (jnp.tile(diff_q_g[l], 2 * DIFF_HEADS) * (DIFF_DH ** -0.5 * LOG2E)).reshape(1, 512)
    p['dk_g'] = jnp.tile(diff_k_g[l], 2 * DIFF_HEADS).reshape(1, 512)
    p['cq_g'] = mla_cq_g[l].reshape(1, MLA_Q_LORA)
    p['ckv_g'] = mla_ckv_g[l].reshape(1, MLA_KV_LORA)
    zpad = jnp.zeros((MLA_HP - MLA_QK,), F32)
    p['mq_g'] = (jnp.tile(jnp.concatenate([mla_q_g[l], zpad]), MLA_HEADS) * (MLA_QK ** -0.5 * LOG2E)).reshape(1, -1)
    p['mk_g'] = jnp.tile(jnp.concatenate([mla_k_g[l][:MLA_NOPE], jnp.zeros((MLA_HP - MLA_NOPE,), F32)]),
                         MLA_HEADS).reshape(1, -1)
    p['kr_g'] = jnp.pad(mla_k_g[l][MLA_NOPE:], (0, LANES - MLA_ROPE)).reshape(1, LANES)
    wq = w_uq[l].reshape(MLA_Q_LORA, MLA_HEADS, MLA_QK)
    p['w_uq'] = jnp.pad(wq, ((0, 0), (0, 0), (0, MLA_HP - MLA_QK))).reshape(MLA_Q_LORA, -1).astype(BF16)
    wkv = w_ukv[l].reshape(MLA_KV_LORA, MLA_HEADS, MLA_NOPE + MLA_V)
    p['w_uk'] = jnp.pad(wkv[:, :, :MLA_NOPE], ((0, 0), (0, 0), (0, MLA_HP - MLA_NOPE))).reshape(
        MLA_KV_LORA, -1).astype(BF16)
    wv = wkv[:, :, MLA_NOPE:].reshape(MLA_KV_LORA, MLA_HEADS // 2, 2, MLA_V)
    zv = jnp.zeros_like(wv[:, :, 0])
    wv = jnp.stack([jnp.concatenate([wv[:, :, 0], zv], axis=-1), jnp.concatenate([zv, wv[:, :, 1]], axis=-1)], axis=2)
    p['w_uv'] = wv.reshape(MLA_KV_LORA, MLA_HEADS * MLA_HP).astype(BF16)
    p['lam'] = diff_lambda[l]
    p['subln_g'] = (diff_subln_g[l] * (1.0 - lam_init)).reshape(1, DIFF_VD)
    p['lam_init'] = lam_init
    p['conv_w'] = jnp.pad(ssd_conv_w[l], ((0, SUBLANES - SSD_CONV), (0, 0)))
    p['conv_b'] = ssd_conv_b[l].reshape(1, -1)
    p['dt_bias'] = jnp.pad(ssd_dt_bias[l].reshape(-1), (0, LANES - 2 * SSD_HEADS)).reshape(1, LANES)
    p['a_log'] = jnp.pad(ssd_A_log[l].reshape(-1), (0, LANES - 2 * SSD_HEADS)).reshape(1, LANES)
    p['d_skip'] = jnp.repeat(ssd_D[l], SSD_P).reshape(1, SSD_INNER)
    p['ssd_g'] = ssd_norm_g[l].reshape(1, SSD_INNER)
    p['w_branch'] = w_branch[l].astype(BF16)
    p['w_out'] = w_out[l].astype(BF16)
    rw = jnp.pad(jnp.concatenate([moe_group_w[l], moe_expert_w[l]], axis=1),
                 ((0, 0), (0, LANES - MOE_GROUPS - MOE_EXPERTS)))
    p['rw_hi'], p['rw_lo'] = _split2(rw)
    p['rb'] = jnp.pad(jnp.concatenate([moe_group_b[l], moe_expert_b[l]]),
                      (0, LANES - MOE_GROUPS - MOE_EXPERTS)).reshape(1, LANES)
    p['w_gu'] = jnp.concatenate([moe_w_gate[l], moe_w_up[l]], axis=-1).astype(BF16)
    p['w_d'] = moe_w_down[l].astype(BF16)
    return p


def kernel(x, c, ctx, c_ctx, ada_w, ada_b, norm1_g, norm2_g, w_in, diff_q_g, diff_k_g, diff_lambda, diff_subln_g, ssd_conv_w, ssd_conv_b, ssd_dt_bias, ssd_A_log, ssd_D, ssd_norm_g, mla_cq_g, mla_ckv_g, w_uq, w_ukv, mla_q_g, mla_k_g, w_branch, w_out, moe_group_w, moe_group_b, moe_expert_w, moe_expert_b, moe_w_gate, moe_w_up, moe_w_down):
    n_batch, n_lat, d = x.shape
    n_ctx = ctx.shape[1]
    depth = w_in.shape[0]
    assert d == D_MODEL and n_ctx == TM and n_lat % TM == 0 and n_lat % GRID_W == 0
    assert n_batch + 1 <= MOD_ROWS

    cc = jnp.concatenate([c, c_ctx[None, :], jnp.zeros((MOD_ROWS - n_batch - 1, d), F32)], axis=0)
    mod = _modulation(cc, ada_w, ada_b)
    tabs = _static_tables(n_ctx, n_lat)
    xs = jnp.concatenate([ctx, x], axis=1)

    for l in range(depth):
        last = l == depth - 1
        t0 = 1 if last else 0
        p = _prep_layer(l, w_in, diff_q_g, diff_k_g, diff_lambda, diff_subln_g, ssd_conv_w, ssd_conv_b,
                        ssd_dt_bias, ssd_A_log, ssd_D, ssd_norm_g, mla_cq_g, mla_ckv_g, w_uq, w_ukv, mla_q_g,
                        mla_k_g, w_branch, w_out, norm1_g, norm2_g, moe_group_w, moe_group_b, moe_expert_w,
                        moe_expert_b, moe_w_gate, moe_w_up, moe_w_down)
        dq, dk, dv, z, xbc, dt, mq, mk, mv, gates = _in_projection(xs, mod[l], p, tabs)
        yf, u = _ssd_scan(xbc, dt, p, n_ctx, 0)
        ssd_o = _ssd_scan(u, dt, p, n_ctx, 1, yf=yf, z=z)
        diff_o = _diff_attention(dq, dk, dv, p['lam'], p['subln_g'], n_ctx, p['lam_init'], t0)
        mla_o = _mla_attention(mq, mk, mv, n_ctx, t0)
        x_mid, h2, route, counts = _merge(xs, diff_o, ssd_o, mla_o, gates, mod[l], p, t0)
        xs = _moe(x_mid, h2, route, counts, mod[l], p, t0)
    return xs
```

```python
import functools
import math

import jax
import jax.numpy as jnp
import numpy as np
from jax import lax
from jax.experimental import pallas as pl
from jax.experimental.pallas import tpu as pltpu

F32 = jnp.float32
BF16 = jnp.bfloat16
I32 = jnp.int32
U32 = jnp.uint32

D_MODEL = 1024
GRID_W = 64
ROPE_BASE = 10000.0
EPS = 1e-6
DIFF_HEADS = 4
DIFF_DH = 64
DIFF_VD = 2 * DIFF_DH
SSD_HEADS = 8
SSD_P = 64
SSD_INNER = SSD_HEADS * SSD_P
SSD_GROUPS = 2
SSD_STATE = 128
SSD_CONV = 5
SSD_CHUNK = 128
MLA_HEADS = 8
MLA_NOPE = 64
MLA_ROPE = 32
MLA_V = 64
MLA_Q_LORA = 384
MLA_KV_LORA = 256
MLA_QK = MLA_NOPE + MLA_ROPE
N_BRANCH = 3
MOE_GROUPS = 4
MOE_PER_GROUP = 8
MOE_EXPERTS = MOE_GROUPS * MOE_PER_GROUP
MOE_TOP_K = 2
MOE_HIDDEN = 256

LANES = 128
SUBLANES = 8
TM = 256
MOE_BM = 256
DIFF_HEADS_PER_STEP = 2
MLA_HEADS_PER_STEP = 4
ATTN_KEY_CHUNK = 512
VMEM_LIMIT = 56 * 1024 * 1024
HALO = SUBLANES
MOD_ROWS = 16
NEG = -1e30
LOG2E = math.log2(math.e)

SEG_DQ = (0, 512)
SEG_DK = (512, 1024)
SEG_DV = (1024, 1536)
SEG_Z = (1536, 2048)
SEG_XBC = (2048, 3072)
SEG_DT = (3072, 3200)
SEG_CQ = (3200, 3584)
SEG_CKV = (3584, 3840)
SEG_KR = (3840, 3968)
SEG_GATES = (3968, 7040)
W_IN_PAD = 7040
MLA_HP = 128

R_E1, R_E2, R_R1, R_R2, R_W1, R_W2 = 0, 1, 2, 3, 4, 5


def _cparams(sem, vmem=VMEM_LIMIT):
    return pltpu.CompilerParams(dimension_semantics=sem, vmem_limit_bytes=vmem)


def _dot(a, b):
    return jnp.dot(a, b, preferred_element_type=F32)


def _dot_nt(a, b):
    return lax.dot_general(a, b, (((1,), (1,)), ((), ())), preferred_element_type=F32)


def _split2(x):
    hi = x.astype(BF16)
    lo = (x - hi.astype(F32)).astype(BF16)
    return hi, lo


def _split3(x):
    hi = x.astype(BF16)
    r = x - hi.astype(F32)
    mid = r.astype(BF16)
    lo = (r - mid.astype(F32)).astype(BF16)
    return hi, mid, lo


def _pack_bf16_pair(x):
    w = x.shape[1] // 2
    bits = lax.bitcast_convert_type(x.astype(BF16).astype(F32), U32)
    return bits[:, :w] | (bits[:, w:] >> 16)


def _unpack_bf16_pair(p):
    hi = lax.bitcast_convert_type(p & jnp.uint32(0xFFFF0000), F32)
    lo = lax.bitcast_convert_type(p << 16, F32)
    return jnp.concatenate([hi, lo], axis=-1)


def _silu(x):
    return x * jax.nn.sigmoid(x)


def _rms(x):
    return x * lax.rsqrt(jnp.mean(x * x, axis=-1, keepdims=True) + EPS)


def _group_rms_scale(x, gmat, emat):
    ms = _dot((x * x).astype(BF16), gmat)
    r = lax.rsqrt(ms + EPS)
    r_hi, r_lo = _split2(r)
    return x * (_dot(r_hi, emat) + _dot(r_lo, emat))


def _rope(x, cos, sin_a, sin_b, quarter):
    w = x.shape[-1]
    rep = lambda t: t if t.shape[-1] == w else jnp.concatenate([t] * (w // t.shape[-1]), axis=-1)
    return x * rep(cos) + pltpu.roll(x, quarter, 1) * rep(sin_a) + pltpu.roll(x, w - quarter, 1) * rep(sin_b)


def _const_spec(shape):
    nd = len(shape)
    return pl.BlockSpec(shape, lambda *_: (0,) * nd, pipeline_mode=pl.Buffered(1))


def _mod_kernel(c_ref, w_ref, b_ref, o_ref):
    s = _silu(c_ref[...])
    o_ref[0] = jnp.dot(s, w_ref[0], precision=lax.Precision.HIGHEST, preferred_element_type=F32) + b_ref[0]


def _modulation(cc, ada_w, ada_b):
    depth, d, n = ada_w.shape
    tn = 1536
    return pl.pallas_call(
        _mod_kernel,
        grid=(depth, n // tn),
        in_specs=[pl.BlockSpec((MOD_ROWS, d), lambda l, j: (0, 0)),
                  pl.BlockSpec((1, d, tn), lambda l, j: (l, 0, j)),
                  pl.BlockSpec((1, 1, tn), lambda l, j: (l, 0, j))],
        out_specs=pl.BlockSpec((1, MOD_ROWS, tn), lambda l, j: (l, 0, j)),
        out_shape=jax.ShapeDtypeStruct((depth, MOD_ROWS, n), F32),
        compiler_params=_cparams(("arbitrary", "arbitrary")),
        name="adaln_mod",
    )(cc, ada_w, ada_b.reshape(depth, 1, n))


def _inproj_kernel(x_ref, mod_ref, g1_ref, w_ref, wuq_ref, wuk_ref, wuv_ref,
                   gd_ref, ed_ref, gq_ref, eq_ref, gk_ref, ek_ref, ekr_ref,
                   cosd_ref, sad_ref, sbd_ref, cosq_ref, saq_ref, sbq_ref,
                   dqg_ref, dkg_ref, cqg_ref, ckvg_ref, mqg_ref, mkg_ref, krg_ref, mvo_ref,
                   dq_ref, dk_ref, dv_ref, z_ref, xbc_ref, dt_ref, mq_ref, mk_ref, mv_ref, gates_ref,
                   *, n_batch):
    t = pl.program_id(0)
    b = pl.program_id(1)
    row = jnp.where(t == 0, n_batch, b)
    shift = mod_ref[pl.ds(row, 1), 0:D_MODEL]
    scale = mod_ref[pl.ds(row, 1), D_MODEL:2 * D_MODEL]
    xn = _rms(x_ref[0]) * g1_ref[...]
    h = (xn * (1.0 + scale) + shift).astype(BF16)

    big = _dot(h, w_ref[...])

    def proj(seg):
        return big[:, seg[0]:seg[1]]

    cosd, sad, sbd = cosd_ref[...], sad_ref[...], sbd_ref[...]
    dq = _group_rms_scale(proj(SEG_DQ), gd_ref[...], ed_ref[...]) * dqg_ref[...]
    dq_ref[0] = _rope(dq, cosd, sad, sbd, DIFF_DH // 4).astype(BF16)
    dk = _group_rms_scale(proj(SEG_DK), gd_ref[...], ed_ref[...]) * dkg_ref[...]
    dk_ref[0] = _rope(dk, cosd, sad, sbd, DIFF_DH // 4).astype(BF16)
    dv_ref[0] = proj(SEG_DV).astype(BF16)
    z_ref[0] = proj(SEG_Z).astype(BF16)
    xbc_ref[0] = proj(SEG_XBC)
    dt_ref[0] = proj(SEG_DT)
    gates_ref[0] = jax.nn.sigmoid(proj(SEG_GATES)).astype(BF16)

    cq = _rms(proj(SEG_CQ)) * cqg_ref[...]
    q = _dot(cq.astype(BF16), wuq_ref[...])
    q = _group_rms_scale(q, gq_ref[...], eq_ref[...]) * mqg_ref[...]
    mq_ref[0] = _rope(q, cosq_ref[...], saq_ref[...], sbq_ref[...], MLA_ROPE // 4).astype(BF16)

    ckv = (_rms(proj(SEG_CKV)) * ckvg_ref[...]).astype(BF16)
    kn = _dot(ckv, wuk_ref[...])
    kn = _group_rms_scale(kn, gk_ref[...], ek_ref[...]) * mkg_ref[...]
    mv_ref[0] = (_dot(ckv, wuv_ref[...]) + mvo_ref[...]).astype(BF16)
    kr = proj(SEG_KR)
    kr = kr * lax.rsqrt(jnp.sum(kr * kr, axis=-1, keepdims=True) * (1.0 / MLA_ROPE) + EPS) * krg_ref[...]
    kr = _rope(kr, cosq_ref[...], saq_ref[...], sbq_ref[...], MLA_ROPE // 4).astype(BF16)
    mk_ref[0] = (kn + _dot(kr, ekr_ref[...])).astype(BF16)


def _rope_tables(n_ctx, n_lat, dim, lane_layout):
    quarter = dim // 4
    inv_freq = ROPE_BASE ** (-jnp.arange(quarter, dtype=F32) / quarter)
    s = jnp.arange(n_lat, dtype=I32)
    ar = (s // GRID_W).astype(F32)[:, None] * inv_freq
    ac = (s % GRID_W).astype(F32)[:, None] * inv_freq
    ang = jnp.concatenate([ar, ar, ac, ac], axis=-1)
    ang = jnp.concatenate([jnp.zeros((n_ctx, dim), F32), ang], axis=0)
    odd = jnp.asarray((np.arange(dim) // quarter) % 2 == 1)
    cos_d, sin_d = jnp.cos(ang), jnp.sin(ang)
    units = (cos_d, jnp.where(odd, sin_d, 0.0), jnp.where(odd, 0.0, -sin_d))
    out = []
    for unit, fill in zip(units, (1.0, 0.0, 0.0)):
        pieces = [unit if item == 'rope' else jnp.full((n_ctx + n_lat, item), fill, F32) for item in lane_layout]
        out.append(jnp.concatenate(pieces, axis=-1))
    return out


def _group_mats(width, groups):
    g = np.zeros((width, LANES), np.float32)
    e = np.zeros((LANES, width), np.float32)
    for i, (st, sz) in enumerate(groups):
        g[st:st + sz, i] = 1.0 / sz
        e[i, st:st + sz] = 1.0
    return jnp.asarray(g, BF16), jnp.asarray(e, BF16)


def _static_tables(n_ctx, n_lat):
    tabs = {}
    tabs['gd'], tabs['ed'] = _group_mats(512, [(i * DIFF_DH, DIFF_DH) for i in range(2 * DIFF_HEADS)])
    qgroups = []
    for h in range(MLA_HEADS):
        qgroups += [(h * MLA_HP, MLA_NOPE), (h * MLA_HP + MLA_NOPE, MLA_ROPE)]
    tabs['gq'], tabs['eq'] = _group_mats(MLA_HEADS * MLA_HP, qgroups)
    tabs['gk'], tabs['ek'] = _group_mats(MLA_HEADS * MLA_HP, [(h * MLA_HP, MLA_NOPE) for h in range(MLA_HEADS)])
    ekr = np.zeros((LANES, MLA_HEADS * MLA_HP), np.float32)
    for h in range(MLA_HEADS):
        for j in range(MLA_ROPE):
            ekr[MLA_NOPE + j, h * MLA_HP + MLA_NOPE + j] = 1.0
    tabs['ekr'] = jnp.asarray(ekr, BF16)
    mv_ones = np.zeros((1, MLA_HEADS * MLA_HP), np.float32)
    for h in range(MLA_HEADS):
        mv_ones[0, h * MLA_HP + (MLA_V if h % 2 == 0 else 0)] = 1.0
    tabs['mv_ones'] = jnp.asarray(mv_ones)
    tabs['cosd'], tabs['sad'], tabs['sbd'] = _rope_tables(n_ctx, n_lat, DIFF_DH, ['rope'] * (LANES // DIFF_DH))
    tabs['cosq'], tabs['saq'], tabs['sbq'] = _rope_tables(n_ctx, n_lat, MLA_ROPE, [MLA_NOPE, 'rope', MLA_HP - MLA_QK])
    return tabs


def _in_projection(xs, mod_l, p, tabs):
    n_batch, n_rows, d = xs.shape
    nt = n_rows // TM
    row_spec = lambda w: pl.BlockSpec((1, TM, w), lambda t, b: (b, t, 0))
    tab_spec = lambda w: pl.BlockSpec((TM, w), lambda t, b: (t, 0))
    consts = [mod_l, p['g1'], p['w_in'], p['w_uq'], p['w_uk'], p['w_uv'],
              tabs['gd'], tabs['ed'], tabs['gq'], tabs['eq'], tabs['gk'], tabs['ek'], tabs['ekr']]
    rope = [tabs['cosd'], tabs['sad'], tabs['sbd'], tabs['cosq'], tabs['saq'], tabs['sbq']]
    gains = [p['dq_g'], p['dk_g'], p['cq_g'], p['ckv_g'], p['mq_g'], p['mk_g'], p['kr_g'], tabs['mv_ones']]
    out_w = [(512, BF16), (512, BF16), (512, BF16), (512, BF16), (1024, F32), (LANES, F32),
             (MLA_HEADS * MLA_HP, BF16), (MLA_HEADS * MLA_HP, BF16), (MLA_HEADS * MLA_HP, BF16),
             (N_BRANCH * D_MODEL, BF16)]
    return pl.pallas_call(
        functools.partial(_inproj_kernel, n_batch=n_batch),
        grid=(nt, n_batch),
        in_specs=([row_spec(d)] + [_const_spec(a.shape) for a in consts]
                  + [tab_spec(a.shape[1]) for a in rope] + [_const_spec(a.shape) for a in gains]),
        out_specs=[row_spec(w) for w, _ in out_w],
        out_shape=[jax.ShapeDtypeStruct((n_batch, n_rows, w), dt) for w, dt in out_w],
        compiler_params=_cparams(("arbitrary", "arbitrary")),
        name="in_proj",
    )(xs, *consts, *rope, *gains)


def _ssd_chunk_of_step(s, direction, n_ctx_chunks, n_chunks):
    if direction == 0:
        return s
    return jnp.where(s < n_ctx_chunks, n_ctx_chunks - 1 - s, n_chunks - 1 + n_ctx_chunks - s)


def _ssd_kernel(*refs, direction, n_ctx_chunks, n_chunks):
    if direction == 0:
        (xbc_ref, prev_ref, next_ref, dt_ref, cw_ref, cb_ref, dtb_ref, alog_ref, y_ref, u_ref, state_ref) = refs
    else:
        (u_ref, dt_ref, dtb_ref, alog_ref, yf_ref, z_ref, dskip_ref, ng_ref, y_ref, state_ref) = refs
    s = pl.program_id(1)
    c = _ssd_chunk_of_step(s, direction, n_ctx_chunks, n_chunks)
    ck = SSD_CHUNK

    @pl.when(s == 0)
    def _():
        state_ref[...] = jnp.zeros_like(state_ref)

    if direction == 0:
        first = jnp.logical_or(c == 0, c == n_ctx_chunks)
        last = jnp.logical_or(c == n_ctx_chunks - 1, c == n_chunks - 1)
        prev = jnp.where(first, 0.0, prev_ref[0])
        nxt = jnp.where(last, 0.0, next_ref[0])
        full = jnp.concatenate([prev, xbc_ref[0], nxt], axis=0)
        n_full = ck + 2 * HALO
        half = (SSD_CONV - 1) // 2
        conv = None
        for k in range(SSD_CONV):
            sh = (half - k) % n_full
            rolled = full if sh == 0 else pltpu.roll(full, sh, 0)
            term = rolled[HALO:HALO + ck] * cw_ref[k:k + 1, :]
            conv = term if conv is None else conv + term
        u = _silu(conv + cb_ref[...])
        u_ref[0] = u.astype(BF16)
    else:
        u = u_ref[0].astype(F32)
    xs = u[:, :SSD_INNER]
    bm = u[:, SSD_INNER:SSD_INNER + SSD_GROUPS * SSD_STATE]
    cm = u[:, SSD_INNER + SSD_GROUPS * SSD_STATE:]

    dt = jax.nn.softplus(dt_ref[0] + dtb_ref[...])
    a = dt * (-jnp.exp(alog_ref[...]))
    ri = lax.broadcasted_iota(I32, (ck, ck), 0)
    ci = lax.broadcasted_iota(I32, (ck, ck), 1)
    valid = (ci <= ri) if direction == 0 else (ci >= ri)
    tri = jnp.where(valid, 1.0, 0.0).astype(BF16)
    a1, a2, a3 = _split3(a)
    acum = _dot(tri, a1) + _dot(tri, a2) + _dot(tri, a3)
    end_row = ck - 1 if direction == 0 else 0
    atot = acum[end_row:end_row + 1, :]
    w_end = jnp.exp(atot - acum) * dt
    e_in = jnp.exp(acum)
    cdec = jnp.exp(atot)
    acum_t = acum.T
    dt_t = dt.T
    lane = lax.broadcasted_iota(I32, (1, LANES), 1)
    left = lane < SSD_P

    pairs_per_group = SSD_HEADS // SSD_GROUPS // 2
    ys = []
    for g in range(SSD_GROUPS):
        bg = bm[:, g * SSD_STATE:(g + 1) * SSD_STATE]
        cg = cm[:, g * SSD_STATE:(g + 1) * SSD_STATE].astype(BF16)
        cb = _dot_nt(cg, bg.astype(BF16))
        bg_t = bg.T.astype(BF16)
        for pp in range(pairs_per_group):
            pr = g * pairs_per_group + pp
            xs_p = xs[:, pr * LANES:(pr + 1) * LANES]
            c0 = direction * SSD_HEADS + 2 * pr
            yd = None
            for side in range(2):
                col = c0 + side
                seg = acum[:, col:col + 1] - acum_t[col:col + 1, :]
                decay = jnp.where(valid, jnp.exp(jnp.where(valid, seg, 0.0)), 0.0)
                mix = (cb * decay * dt_t[col:col + 1, :]).astype(BF16)
                keep = left if side == 0 else jnp.logical_not(left)
                part = _dot(mix, jnp.where(keep, xs_p, 0.0).astype(BF16))
                yd = part if yd is None else yd + part
            pick = lambda m: jnp.where(left, m[:, c0:c0 + 1], m[:, c0 + 1:c0 + 2])
            h_in = state_ref[pr]
            y_off = _dot(cg, h_in.astype(BF16)) * pick(e_in)
            xw = (xs_p * pick(w_end)).astype(BF16)
            state_ref[pr] = h_in * pick(cdec) + _dot(bg_t, xw)
            ys.append(yd + y_off)
    y = jnp.concatenate(ys, axis=-1)
    if direction == 0:
        y_ref[0] = y
    else:
        y = y + yf_ref[0] + dskip_ref[...] * xs
        y = y * _silu(z_ref[0].astype(F32))
        y_ref[0] = (_rms(y) * ng_ref[...]).astype(BF16)


def _ssd_scan(xbc, dt, p, n_ctx, direction, yf=None, z=None):
    n_batch, n_rows, wc = xbc.shape
    ck = SSD_CHUNK
    n_chunks = n_rows // ck
    n_ctx_chunks = n_ctx // ck
    n_halo_blocks = n_rows // HALO
    per = ck // HALO
    cmap = lambda s: _ssd_chunk_of_step(s, direction, n_ctx_chunks, n_chunks)
    cur = lambda w: pl.BlockSpec((1, ck, w), lambda b, s: (b, cmap(s), 0))
    if direction == 0:
        in_specs = [cur(wc),
                    pl.BlockSpec((1, HALO, wc), lambda b, s: (b, jnp.maximum(cmap(s) * per - 1, 0), 0)),
                    pl.BlockSpec((1, HALO, wc),
                                 lambda b, s: (b, jnp.minimum((cmap(s) + 1) * per, n_halo_blocks - 1), 0)),
                    cur(LANES),
                    _const_spec(p['conv_w'].shape), _const_spec(p['conv_b'].shape),
                    _const_spec(p['dt_bias'].shape), _const_spec(p['a_log'].shape)]
        args = [xbc, xbc, xbc, dt, p['conv_w'], p['conv_b'], p['dt_bias'], p['a_log']]
        out_specs = [cur(SSD_INNER), cur(wc)]
        out_shape = [jax.ShapeDtypeStruct((n_batch, n_rows, SSD_INNER), F32),
                     jax.ShapeDtypeStruct((n_batch, n_rows, wc), BF16)]
    else:
        in_specs = [cur(wc), cur(LANES), _const_spec(p['dt_bias'].shape), _const_spec(p['a_log'].shape),
                    cur(SSD_INNER), cur(SSD_INNER), _const_spec(p['d_skip'].shape), _const_spec(p['ssd_g'].shape)]
        args = [xbc, dt, p['dt_bias'], p['a_log'], yf, z, p['d_skip'], p['ssd_g']]
        out_specs = cur(SSD_INNER)
        out_shape = jax.ShapeDtypeStruct((n_batch, n_rows, SSD_INNER), BF16)
    return pl.pallas_call(
        functools.partial(_ssd_kernel, direction=direction, n_ctx_chunks=n_ctx_chunks, n_chunks=n_chunks),
        grid=(n_batch, n_chunks),
        in_specs=in_specs,
        out_specs=out_specs,
        out_shape=out_shape,
        scratch_shapes=[pltpu.VMEM((SSD_HEADS // 2, SSD_STATE, LANES), F32)],
        compiler_params=_cparams(("arbitrary", "arbitrary")),
        name="ssd_fwd" if direction == 0 else "ssd_bwd",
    )(*args)


def _softmax_parts(s2):
    m = jnp.max(s2, axis=-1, keepdims=True)
    e = jnp.exp2(s2 - m)
    return e, jnp.sum(e, axis=-1, keepdims=True)


def _softmax_numerator(s2):
    m = jnp.max(s2, axis=-1, keepdims=True)
    return jnp.exp2((s2 - m).astype(BF16))


def _diff_attn_kernel(q_ref, k_ref, v_ref, lam_ref, sg_ref, o_ref, *, n_ctx, lam_init, t0):
    lv = lam_ref[...]
    lam = (jnp.exp(jnp.sum(lv[0:1] * lv[1:2], axis=-1, keepdims=True))
           - jnp.exp(jnp.sum(lv[2:3] * lv[3:4], axis=-1, keepdims=True)) + lam_init)
    lane = lax.broadcasted_iota(I32, (1, DIFF_VD), 1)

    def attend(n_keys):
        for hd in range(DIFF_HEADS_PER_STEP):
            sl = slice(hd * DIFF_VD, (hd + 1) * DIFF_VD)
            q = q_ref[0, :, sl]
            q0 = jnp.where(lane < DIFF_DH, q, jnp.zeros_like(q))
            q1 = jnp.where(lane < DIFF_DH, jnp.zeros_like(q), q)
            k = k_ref[0, 0:n_keys, sl]
            e0, l0 = _softmax_parts(_dot_nt(q0, k))
            e1, l1 = _softmax_parts(_dot_nt(q1, k))
            w = (e0 - (lam * l0 / l1) * e1).astype(BF16)
            o = _dot(w, v_ref[0, 0:n_keys, sl]) * (1.0 / l0)
            o_ref[0, :, sl] = (_rms(o) * sg_ref[...]).astype(BF16)

    is_ctx = t0 + pl.program_id(2) == 0

    @pl.when(is_ctx)
    def _():
        attend(n_ctx)

    @pl.when(jnp.logical_not(is_ctx))
    def _():
        attend(k_ref.shape[1])


def _diff_attention(dq, dk, dv, lam_p, subln_g, n_ctx, lam_init, t0):
    n_batch, n_rows, _ = dq.shape
    nq = n_rows // TM - t0
    return pl.pallas_call(
        functools.partial(_diff_attn_kernel, n_ctx=n_ctx, lam_init=lam_init, t0=t0),
        grid=(n_batch, DIFF_HEADS // DIFF_HEADS_PER_STEP, nq),
        in_specs=[pl.BlockSpec((1, TM, DIFF_HEADS_PER_STEP * DIFF_VD), lambda b, h, t: (b, t0 + t, h)),
                  pl.BlockSpec((1, n_rows, DIFF_HEADS_PER_STEP * DIFF_VD), lambda b, h, t: (b, 0, h)),
                  pl.BlockSpec((1, n_rows, DIFF_HEADS_PER_STEP * DIFF_VD), lambda b, h, t: (b, 0, h)),
                  _const_spec(lam_p.shape), _const_spec(subln_g.shape)],
        out_specs=pl.BlockSpec((1, TM, DIFF_HEADS_PER_STEP * DIFF_VD), lambda b, h, t: (b, t, h)),
        out_shape=jax.ShapeDtypeStruct((n_batch, nq * TM, DIFF_HEADS * DIFF_VD), BF16),
        compiler_params=_cparams(("arbitrary", "arbitrary", "arbitrary")),
        name="diff_attn",
    )(dq, dk, dv, lam_p, subln_g)


def _mla_attn_kernel(q_ref, k_ref, v_ref, o_ref, *, n_ctx, t0):
    lane = lax.broadcasted_iota(I32, (1, MLA_HP), 1)

    def attend(n_keys):
        res = []
        for hd in range(MLA_HEADS_PER_STEP):
            sl = slice(hd * MLA_HP, (hd + 1) * MLA_HP)
            e = _softmax_numerator(_dot_nt(q_ref[0, :, sl], k_ref[0, 0:n_keys, sl]))
            res.append(_dot(e, v_ref[0, 0:n_keys, sl]))
        for pr in range(MLA_HEADS_PER_STEP // 2):
            r0, r1 = res[2 * pr], res[2 * pr + 1]
            o0 = r0 * (1.0 / r0[:, MLA_V:MLA_V + 1])
            o1 = r1 * (1.0 / r1[:, 0:1])
            o_ref[0, :, pr * MLA_HP:(pr + 1) * MLA_HP] = jnp.where(lane < MLA_V, o0, o1).astype(BF16)

    is_ctx = t0 + pl.program_id(2) == 0

    @pl.when(is_ctx)
    def _():
        attend(n_ctx)

    @pl.when(jnp.logical_not(is_ctx))
    def _():
        attend(k_ref.shape[1])


def _mla_attention(mq, mk, mv, n_ctx, t0):
    n_batch, n_rows, _ = mq.shape
    nq = n_rows // TM - t0
    return pl.pallas_call(
        functools.partial(_mla_attn_kernel, n_ctx=n_ctx, t0=t0),
        grid=(n_batch, MLA_HEADS // MLA_HEADS_PER_STEP, nq),
        in_specs=[pl.BlockSpec((1, TM, MLA_HEADS_PER_STEP * MLA_HP), lambda b, h, t: (b, t0 + t, h)),
                  pl.BlockSpec((1, n_rows, MLA_HEADS_PER_STEP * MLA_HP), lambda b, h, t: (b, 0, h)),
                  pl.BlockSpec((1, n_rows, MLA_HEADS_PER_STEP * MLA_HP), lambda b, h, t: (b, 0, h))],
        out_specs=pl.BlockSpec((1, TM, MLA_HEADS_PER_STEP * MLA_V), lambda b, h, t: (b, t, h)),
        out_shape=jax.ShapeDtypeStruct((n_batch, nq * TM, MLA_HEADS * MLA_V), BF16),
        compiler_params=_cparams(("arbitrary", "arbitrary", "arbitrary")),
        name="mla_attn",
    )(mq, mk, mv)


def _merge_kernel(x_ref, da_ref, ss_ref, ma_ref, gates_ref, mod_ref, wb_ref, wo_ref, g2_ref,
                  rwh_ref, rwl_ref, rb_ref,
                  xo_ref, h2_ref, route_ref, cnt_ref, base_ref, *, n_batch, t0):
    b = pl.program_id(0)
    t = pl.program_id(1)

    @pl.when(jnp.logical_and(b == 0, t == 0))
    def _():
        base_ref[...] = jnp.zeros_like(base_ref)

    row = jnp.where(t0 + t == 0, n_batch, b)
    gate1 = mod_ref[pl.ds(row, 1), 2 * D_MODEL:3 * D_MODEL]
    shift2 = mod_ref[pl.ds(row, 1), 3 * D_MODEL:4 * D_MODEL]
    scale2 = mod_ref[pl.ds(row, 1), 4 * D_MODEL:5 * D_MODEL]

    acc = None
    for k, o_ref in enumerate((da_ref, ss_ref, ma_ref)):
        term = gates_ref[0, :, k * D_MODEL:(k + 1) * D_MODEL].astype(F32) * _dot(o_ref[0], wb_ref[k])
        acc = term if acc is None else acc + term
    x = x_ref[0] + gate1 * _dot(acc.astype(BF16), wo_ref[...])
    xo_ref[0] = x
    h2 = _rms(x) * g2_ref[...] * (1.0 + scale2) + shift2
    h2_ref[0] = _pack_bf16_pair(h2)

    h_hi, h_lo = _split2(h2)
    lg = _dot(h_hi, rwh_ref[...]) + _dot(h_lo, rwh_ref[...]) + _dot(h_hi, rwl_ref[...]) + rb_ref[...]
    lane = lax.broadcasted_iota(I32, lg.shape, 1)
    lane_f = lane.astype(F32)
    big = float(LANES)
    is_g = lane < MOE_GROUPS
    gl = jnp.where(is_g, lg, NEG)
    gmax = jnp.max(gl, axis=-1, keepdims=True)
    gidx = jnp.min(jnp.where(gl == gmax, lane_f, big), axis=-1, keepdims=True)
    pg = 1.0 / jnp.sum(jnp.where(is_g, jnp.exp(gl - gmax), 0.0), axis=-1, keepdims=True)
    lo_lane = MOE_GROUPS + gidx * MOE_PER_GROUP
    in_grp = jnp.logical_and(lane_f >= lo_lane, lane_f < lo_lane + MOE_PER_GROUP)
    el = jnp.where(in_grp, lg, NEG)
    m1 = jnp.max(el, axis=-1, keepdims=True)
    i1 = jnp.min(jnp.where(el == m1, lane_f, big), axis=-1, keepdims=True)
    el2 = jnp.where(lane_f == i1, NEG, el)
    m2 = jnp.max(el2, axis=-1, keepdims=True)
    i2 = jnp.min(jnp.where(el2 == m2, lane_f, big), axis=-1, keepdims=True)
    tt = jnp.exp(m2 - m1)
    w1 = pg / (1.0 + tt)
    w2 = pg * tt / (1.0 + tt)
    e1 = i1 - MOE_GROUPS
    e2 = i2 - MOE_GROUPS

    oh1 = lane_f == e1
    oh2 = lane_f == e2
    oh = jnp.where(jnp.logical_or(oh1, oh2), 1.0, 0.0)
    tm = oh.shape[0]
    ri = lax.broadcasted_iota(I32, (tm, tm), 0)
    ci = lax.broadcasted_iota(I32, (tm, tm), 1)
    strict = jnp.where(ci < ri, 1.0, 0.0).astype(BF16)
    base = base_ref[0:1, :]
    rank_all = _dot(strict, oh.astype(BF16)) + base
    r1 = jnp.sum(jnp.where(oh1, rank_all, 0.0), axis=-1, keepdims=True)
    r2 = jnp.sum(jnp.where(oh2, rank_all, 0.0), axis=-1, keepdims=True)
    new_base = base + jnp.sum(oh, axis=0, keepdims=True)
    base_ref[...] = jnp.broadcast_to(new_base, base_ref.shape)
    cnt_ref[...] = jnp.broadcast_to(new_base, cnt_ref.shape)

    rec = jnp.zeros(lg.shape, F32)
    for ln, val in ((R_E1, e1), (R_E2, e2), (R_R1, r1), (R_R2, r2), (R_W1, w1), (R_W2, w2)):
        rec = jnp.where(lane == ln, val, rec)
    route_ref[0] = rec


def _merge(xs, da, ss, ma, gates, mod_l, p, t0):
    n_batch, n_rows, d = xs.shape
    nt = n_rows // TM - t0
    rows_out = nt * TM
    in_row = lambda w: pl.BlockSpec((1, TM, w), lambda b, t: (b, t0 + t, 0))
    out_row = lambda w: pl.BlockSpec((1, TM, w), lambda b, t: (b, t, 0))
    consts = [mod_l, p['w_branch'], p['w_out'], p['g2'], p['rw_hi'], p['rw_lo'], p['rb']]
    return pl.pallas_call(
        functools.partial(_merge_kernel, n_batch=n_batch, t0=t0),
        grid=(n_batch, nt),
        in_specs=[in_row(d), out_row(512), in_row(512), out_row(512), in_row(N_BRANCH * D_MODEL)]
                 + [_const_spec(a.shape) for a in consts],
        out_specs=[out_row(d), out_row(d // 2), out_row(LANES), pl.BlockSpec((SUBLANES, LANES), lambda b, t: (0, 0))],
        out_shape=[jax.ShapeDtypeStruct((n_batch, rows_out, d), F32),
                   jax.ShapeDtypeStruct((n_batch, rows_out, d // 2), U32),
                   jax.ShapeDtypeStruct((n_batch, rows_out, LANES), F32),
                   jax.ShapeDtypeStruct((SUBLANES, LANES), F32)],
        scratch_shapes=[pltpu.VMEM((SUBLANES, LANES), F32)],
        compiler_params=_cparams(("arbitrary", "arbitrary")),
        name="merge_route",
    )(xs, da, ss, ma, gates, *consts)


def _row_copy(src_ref, src_row, dst_ref, dst_row, sem):
    return pltpu.make_async_copy(src_ref.at[pl.ds(src_row, 1), :], dst_ref.at[pl.ds(dst_row, 1), :], sem)


def _slot(seg_ref, ids_ref, which, r, tm):
    return seg_ref[ids_ref[0, 0, which * tm + r]] + ids_ref[0, 0, (MOE_TOP_K + which) * tm + r]


def _dispatch_kernel(seg_ref, ids_ref, h2_ref, xs_ref, zero_ref, sem, zsem):
    tm = h2_ref.shape[1]
    src = h2_ref.at[0]

    @pl.when(jnp.logical_and(pl.program_id(0) == 0, pl.program_id(1) == 0))
    def _():
        zero_ref[...] = jnp.zeros_like(zero_ref)

        def zero_tail(e, carry):
            start = seg_ref[e]
            end = seg_ref[e + 1]

            @pl.when(end > start)
            def _():
                cp = pltpu.make_async_copy(
                    zero_ref, xs_ref.at[pl.ds(pl.multiple_of(end - MOE_BM, MOE_BM), MOE_BM), :], zsem)
                cp.start()
                cp.wait()
            return carry

        lax.fori_loop(0, MOE_EXPERTS, zero_tail, 0)

        def zero_unused(j, carry):
            cp = pltpu.make_async_copy(zero_ref, xs_ref.at[pl.ds(pl.multiple_of(j * MOE_BM, MOE_BM), MOE_BM), :], zsem)
            cp.start()
            cp.wait()
            return carry

        lax.fori_loop(seg_ref[MOE_EXPERTS] // MOE_BM, xs_ref.shape[0] // MOE_BM, zero_unused, 0)

    def issue(r, carry):
        for which in range(MOE_TOP_K):
            _row_copy(src, r, xs_ref, _slot(seg_ref, ids_ref, which, r, tm), sem).start(priority=which)
        return carry

    lax.fori_loop(0, tm, issue, 0, unroll=8)
    for _ in range(MOE_TOP_K):
        pltpu.make_async_copy(src, xs_ref.at[pl.ds(0, tm), :], sem).wait()


def _dispatch(h2, seg_start, ids, n_slots):
    n_batch, n_rows, d = h2.shape
    nt = n_rows // TM
    grid_spec = pltpu.PrefetchScalarGridSpec(
        num_scalar_prefetch=1,
        grid=(n_batch, nt),
        in_specs=[pl.BlockSpec((1, 1, 2 * MOE_TOP_K * TM), lambda b, t, seg: (b * nt + t, 0, 0),
                               memory_space=pltpu.SMEM),
                  pl.BlockSpec((1, TM, d), lambda b, t, seg: (b, t, 0))],
        out_specs=pl.BlockSpec(memory_space=pl.ANY),
        scratch_shapes=[pltpu.VMEM((MOE_BM, d), U32), pltpu.SemaphoreType.DMA(()), pltpu.SemaphoreType.DMA(())],
    )
    return pl.pallas_call(
        _dispatch_kernel,
        grid_spec=grid_spec,
        out_shape=jax.ShapeDtypeStruct((n_slots, d), U32),
        compiler_params=_cparams(("arbitrary", "arbitrary")),
        name="moe_dispatch",
    )(seg_start, ids, h2)


def _expert_kernel(be_ref, nu_ref, xs_ref, wgu_ref, wd_ref, y_ref):
    i = pl.program_id(0)

    @pl.when(i < nu_ref[0])
    def _():
        gu = _dot(_unpack_bf16_pair(xs_ref[...]).astype(BF16), wgu_ref[0])
        hid = _silu(gu[:, :MOE_HIDDEN]) * gu[:, MOE_HIDDEN:]
        y_ref[...] = _pack_bf16_pair(_dot(hid.astype(BF16), wd_ref[0]))

    @pl.when(i >= nu_ref[0])
    def _():
        y_ref[...] = jnp.zeros_like(y_ref)


def _expert_ffn(xs, blk_expert, n_used, w_gu, w_d):
    n_slots, dp = xs.shape
    n_blocks = n_slots // MOE_BM
    grid_spec = pltpu.PrefetchScalarGridSpec(
        num_scalar_prefetch=2,
        grid=(n_blocks,),
        in_specs=[pl.BlockSpec((MOE_BM, dp), lambda i, be, nu: (jnp.minimum(i, nu[0] - 1), 0)),
                  pl.BlockSpec((1, D_MODEL, 2 * MOE_HIDDEN), lambda i, be, nu: (be[i], 0, 0)),
                  pl.BlockSpec((1, MOE_HIDDEN, D_MODEL), lambda i, be, nu: (be[i], 0, 0))],
        out_specs=pl.BlockSpec((MOE_BM, dp), lambda i, be, nu: (i, 0)),
    )
    return pl.pallas_call(
        _expert_kernel,
        grid_spec=grid_spec,
        out_shape=jax.ShapeDtypeStruct((n_slots, dp), U32),
        compiler_params=_cparams(("arbitrary",)),
        name="moe_experts",
    )(blk_expert, n_used, xs, w_gu, w_d)


def _combine_kernel(seg_ref, ids_ref, x_ref, route_ref, mod_ref, y_ref, o_ref, buf_ref, sem, *, n_batch, t0):
    b = pl.program_id(0)
    t = pl.program_id(1)
    tm = x_ref.shape[1]

    def issue(r, carry):
        for which in range(MOE_TOP_K):
            _row_copy(y_ref, _slot(seg_ref, ids_ref, which, r, tm), buf_ref.at[which], r, sem).start(priority=which)
        return carry

    lax.fori_loop(0, tm, issue, 0, unroll=8)
    for which in range(MOE_TOP_K):
        pltpu.make_async_copy(y_ref.at[pl.ds(0, tm), :], buf_ref.at[which], sem).wait()

    row = jnp.where(t0 + t == 0, n_batch, b)
    gate2 = mod_ref[pl.ds(row, 1), 5 * D_MODEL:6 * D_MODEL]
    rec = route_ref[0]
    f = (rec[:, R_W1:R_W1 + 1] * _unpack_bf16_pair(buf_ref[0])
         + rec[:, R_W2:R_W2 + 1] * _unpack_bf16_pair(buf_ref[1]))
    o_ref[0] = x_ref[0] + gate2 * f


def _combine(x_mid, route, seg_start, ids, y, mod_l, t0):
    n_batch, n_rows, d = x_mid.shape
    nt = n_rows // TM
    row = lambda w: pl.BlockSpec((1, TM, w), lambda b, t, seg: (b, t, 0))
    grid_spec = pltpu.PrefetchScalarGridSpec(
        num_scalar_prefetch=1,
        grid=(n_batch, nt),
        in_specs=[pl.BlockSpec((1, 1, 2 * MOE_TOP_K * TM), lambda b, t, seg: (b * nt + t, 0, 0),
                               memory_space=pltpu.SMEM),
                  row(d), row(LANES),
                  pl.BlockSpec(mod_l.shape, lambda b, t, seg: (0, 0), pipeline_mode=pl.Buffered(1)),
                  pl.BlockSpec(memory_space=pl.ANY)],
        out_specs=row(d),
        scratch_shapes=[pltpu.VMEM((MOE_TOP_K, TM, d // 2), U32), pltpu.SemaphoreType.DMA(())],
    )
    return pl.pallas_call(
        functools.partial(_combine_kernel, n_batch=n_batch, t0=t0),
        grid_spec=grid_spec,
        out_shape=jax.ShapeDtypeStruct((n_batch, n_rows, d), F32),
        compiler_params=_cparams(("arbitrary", "arbitrary")),
        name="moe_combine",
    )(seg_start, ids, x_mid, route, mod_l, y)


def _moe(x_mid, h2, route, counts, mod_l, p, t0):
    n_batch, n_rows, d = h2.shape
    n_tok = n_batch * n_rows
    n_blocks = -(-(n_tok * MOE_TOP_K) // MOE_BM) + MOE_EXPERTS
    cnt = counts[0, :MOE_EXPERTS].astype(I32)
    padded = (cnt + MOE_BM - 1) // MOE_BM * MOE_BM
    pad_end = jnp.cumsum(padded)
    seg_start = jnp.concatenate([jnp.zeros((1,), I32), pad_end]).astype(I32)
    blk_first = jnp.arange(n_blocks, dtype=I32) * MOE_BM
    blk_expert = jnp.minimum(jnp.sum((pad_end[None, :] <= blk_first[:, None]).astype(I32), axis=1), MOE_EXPERTS - 1)
    n_used = (pad_end[-1:] // MOE_BM).astype(I32)
    ids = route[..., R_E1:R_R2 + 1].astype(I32).reshape(n_tok // TM, TM, 2 * MOE_TOP_K)
    ids = jnp.swapaxes(ids, 1, 2).reshape(n_tok // TM, 1, 2 * MOE_TOP_K * TM)
    xs = _dispatch(h2, seg_start, ids, n_blocks * MOE_BM)
    y = _expert_ffn(xs, blk_expert, n_used, p['w_gu'], p['w_d'])
    return _combine(x_mid, route, seg_start, ids, y, mod_l, t0)


def _prep_layer(l, w_in, diff_q_g, diff_k_g, diff_lambda, diff_subln_g, ssd_conv_w, ssd_conv_b, ssd_dt_bias,
                ssd_A_log, ssd_D, ssd_norm_g, mla_cq_g, mla_ckv_g, w_uq, w_ukv, mla_q_g, mla_k_g, w_branch, w_out,
                norm1_g, norm2_g, moe_group_w, moe_group_b, moe_expert_w, moe_expert_b, moe_w_gate, moe_w_up,
                moe_w_down):
    d = D_MODEL
    lam_init = 0.8 - 0.6 * math.exp(-0.3 * l)
    wi = w_in[l]
    sizes = (512, 512, 512, 512, 1024, 16, MLA_Q_LORA, MLA_KV_LORA, MLA_ROPE, N_BRANCH * d)
    offs = np.concatenate([[0], np.cumsum(sizes)])
    segs = (SEG_DQ, SEG_DK, SEG_DV, SEG_Z, SEG_XBC, SEG_DT, SEG_CQ, SEG_CKV, SEG_KR, SEG_GATES)
    cols = []
    for i, (st, en) in enumerate(segs):
        piece = wi[:, offs[i]:offs[i + 1]]
        left = MLA_NOPE if (st, en) == SEG_KR else 0
        cols.append(jnp.pad(piece, ((0, 0), (left, (en - st) - sizes[i] - left))))
    p = {'w_in': jnp.concatenate(cols, axis=1).astype(BF16)}
    p['g1'] = norm1_g[l].reshape(1, d)
    p['g2'] = norm2_g[l].reshape(1, d)
    p['dq_g'] = (jnp.tile(diff_q_g[l], 2 * DIFF_HEADS) * (DIFF_DH ** -0.5 * LOG2E)).reshape(1, 512)
    p['dk_g'] = jnp.tile(diff_k_g[l], 2 * DIFF_HEADS).reshape(1, 512)
    p['cq_g'] = mla_cq_g[l].reshape(1, MLA_Q_LORA)
    p['ckv_g'] = mla_ckv_g[l].reshape(1, MLA_KV_LORA)
    zpad = jnp.zeros((MLA_HP - MLA_QK,), F32)
    p['mq_g'] = (jnp.tile(jnp.concatenate([mla_q_g[l], zpad]), MLA_HEADS) * (MLA_QK ** -0.5 * LOG2E)).reshape(1, -1)
    p['mk_g'] = jnp.tile(jnp.concatenate([mla_k_g[l][:MLA_NOPE], jnp.zeros((MLA_HP - MLA_NOPE,), F32)]),
                         MLA_HEADS).reshape(1, -1)
    p['kr_g'] = jnp.pad(mla_k_g[l][MLA_NOPE:], (MLA_NOPE, LANES - MLA_QK)).reshape(1, LANES)
    wq = w_uq[l].reshape(MLA_Q_LORA, MLA_HEADS, MLA_QK)
    p['w_uq'] = jnp.pad(wq, ((0, 0), (0, 0), (0, MLA_HP - MLA_QK))).reshape(MLA_Q_LORA, -1).astype(BF16)
    wkv = w_ukv[l].reshape(MLA_KV_LORA, MLA_HEADS, MLA_NOPE + MLA_V)
    p['w_uk'] = jnp.pad(wkv[:, :, :MLA_NOPE], ((0, 0), (0, 0), (0, MLA_HP - MLA_NOPE))).reshape(
        MLA_KV_LORA, -1).astype(BF16)
    wv = wkv[:, :, MLA_NOPE:].reshape(MLA_KV_LORA, MLA_HEADS // 2, 2, MLA_V)
    zv = jnp.zeros_like(wv[:, :, 0])
    wv = jnp.stack([jnp.concatenate([wv[:, :, 0], zv], axis=-1), jnp.concatenate([zv, wv[:, :, 1]], axis=-1)], axis=2)
    p['w_uv'] = wv.reshape(MLA_KV_LORA, MLA_HEADS * MLA_HP).astype(BF16)
    p['lam'] = diff_lambda[l]
    p['subln_g'] = (diff_subln_g[l] * (1.0 - lam_init)).reshape(1, DIFF_VD)
    p['lam_init'] = lam_init
    p['conv_w'] = jnp.pad(ssd_conv_w[l], ((0, SUBLANES - SSD_CONV), (0, 0)))
    p['conv_b'] = ssd_conv_b[l].reshape(1, -1)
    p['dt_bias'] = jnp.pad(ssd_dt_bias[l].reshape(-1), (0, LANES - 2 * SSD_HEADS)).reshape(1, LANES)
    p['a_log'] = jnp.pad(ssd_A_log[l].reshape(-1), (0, LANES - 2 * SSD_HEADS)).reshape(1, LANES)
    p['d_skip'] = jnp.repeat(ssd_D[l], SSD_P).reshape(1, SSD_INNER)
    p['ssd_g'] = ssd_norm_g[l].reshape(1, SSD_INNER)
    p['w_branch'] = w_branch[l].astype(BF16)
    p['w_out'] = w_out[l].astype(BF16)
    rw = jnp.pad(jnp.concatenate([moe_group_w[l], moe_expert_w[l]], axis=1),
                 ((0, 0), (0, LANES - MOE_GROUPS - MOE_EXPERTS)))
    p['rw_hi'], p['rw_lo'] = _split2(rw)
    p['rb'] = jnp.pad(jnp.concatenate([moe_group_b[l], moe_expert_b[l]]),
                      (0, LANES - MOE_GROUPS - MOE_EXPERTS)).reshape(1, LANES)
    p['w_gu'] = jnp.concatenate([moe_w_gate[l], moe_w_up[l]], axis=-1).astype(BF16)
    p['w_d'] = moe_w_down[l].astype(BF16)
    return p


def kernel(x, c, ctx, c_ctx, ada_w, ada_b, norm1_g, norm2_g, w_in, diff_q_g, diff_k_g, diff_lambda, diff_subln_g, ssd_conv_w, ssd_conv_b, ssd_dt_bias, ssd_A_log, ssd_D, ssd_norm_g, mla_cq_g, mla_ckv_g, w_uq, w_ukv, mla_q_g, mla_k_g, w_branch, w_out, moe_group_w, moe_group_b, moe_expert_w, moe_expert_b, moe_w_gate, moe_w_up, moe_w_down):
    n_batch, n_lat, d = x.shape
    n_ctx = ctx.shape[1]
    depth = w_in.shape[0]
    assert d == D_MODEL and n_ctx == TM and n_lat % TM == 0 and n_lat % GRID_W == 0
    assert n_batch + 1 <= MOD_ROWS

    cc = jnp.concatenate([c, c_ctx[None, :], jnp.zeros((MOD_ROWS - n_batch - 1, d), F32)], axis=0)
    mod = _modulation(cc, ada_w, ada_b)
    tabs = _static_tables(n_ctx, n_lat)
    xs = jnp.concatenate([ctx, x], axis=1)

    for l in range(depth):
        last = l == depth - 1
        t0 = 1 if last else 0
        p = _prep_layer(l, w_in, diff_q_g, diff_k_g, diff_lambda, diff_subln_g, ssd_conv_w, ssd_conv_b,
                        ssd_dt_bias, ssd_A_log, ssd_D, ssd_norm_g, mla_cq_g, mla_ckv_g, w_uq, w_ukv, mla_q_g,
                        mla_k_g, w_branch, w_out, norm1_g, norm2_g, moe_group_w, moe_group_b, moe_expert_w,
                        moe_expert_b, moe_w_gate, moe_w_up, moe_w_down)
        dq, dk, dv, z, xbc, dt, mq, mk, mv, gates = _in_projection(xs, mod[l], p, tabs)
        yf, u = _ssd_scan(xbc, dt, p, n_ctx, 0)
        ssd_o = _ssd_scan(u, dt, p, n_ctx, 1, yf=yf, z=z)
        diff_o = _diff_attention(dq, dk, dv, p['lam'], p['subln_g'], n_ctx, p['lam_init'], t0)
        mla_o = _mla_attention(mq, mk, mv, n_ctx, t0)
        x_mid, h2, route, counts = _merge(xs, diff_o, ssd_o, mla_o, gates, mod[l], p, t0)
        xs = _moe(x_mid, h2, route, counts, mod[l], p, t0)
    return xs
```

```python
import functools
import math

import jax
import jax.numpy as jnp
import numpy as np
from jax import lax
from jax.experimental import pallas as pl
from jax.experimental.pallas import tpu as pltpu

F32 = jnp.float32
BF16 = jnp.bfloat16
I32 = jnp.int32
U32 = jnp.uint32

D_MODEL = 1024
GRID_W = 64
ROPE_BASE = 10000.0
EPS = 1e-6
DIFF_HEADS = 4
DIFF_DH = 64
DIFF_VD = 2 * DIFF_DH
SSD_HEADS = 8
SSD_P = 64
SSD_INNER = SSD_HEADS * SSD_P
SSD_GROUPS = 2
SSD_STATE = 128
SSD_CONV = 5
SSD_CHUNK = 128
MLA_HEADS = 8
MLA_NOPE = 64
MLA_ROPE = 32
MLA_V = 64
MLA_Q_LORA = 384
MLA_KV_LORA = 256
MLA_QK = MLA_NOPE + MLA_ROPE
N_BRANCH = 3
MOE_GROUPS = 4
MOE_PER_GROUP = 8
MOE_EXPERTS = MOE_GROUPS * MOE_PER_GROUP
MOE_TOP_K = 2
MOE_HIDDEN = 256

LANES = 128
SUBLANES = 8
TM = 256
MOE_BM = 256
ROW_DMA_UNROLL = 8
DIFF_HEADS_PER_STEP = 2
MLA_HEADS_PER_STEP = 4
ATTN_KEY_CHUNK = 512
VMEM_LIMIT = 56 * 1024 * 1024
HALO = SUBLANES
MOD_ROWS = 16
NEG = -1e30
LOG2E = math.log2(math.e)

SEG_DQ = (0, 512)
SEG_DK = (512, 1024)
SEG_DV = (1024, 1536)
SEG_Z = (1536, 2048)
SEG_XBC = (2048, 3072)
SEG_DT = (3072, 3200)
SEG_CQ = (3200, 3584)
SEG_CKV = (3584, 3840)
SEG_KR = (3840, 3968)
SEG_GATES = (3968, 7040)
W_IN_PAD = 7040
MLA_HP = 128

R_E1, R_E2, R_R1, R_R2, R_W1, R_W2 = 0, 1, 2, 3, 4, 5


def _cparams(sem, vmem=VMEM_LIMIT):
    return pltpu.CompilerParams(dimension_semantics=sem, vmem_limit_bytes=vmem)


def _dot(a, b):
    return jnp.dot(a, b, preferred_element_type=F32)


def _dot_nt(a, b):
    return lax.dot_general(a, b, (((1,), (1,)), ((), ())), preferred_element_type=F32)


def _split2(x):
    hi = x.astype(BF16)
    lo = (x - hi.astype(F32)).astype(BF16)
    return hi, lo


def _split3(x):
    hi = x.astype(BF16)
    r = x - hi.astype(F32)
    mid = r.astype(BF16)
    lo = (r - mid.astype(F32)).astype(BF16)
    return hi, mid, lo


def _pack_bf16_pair(x):
    w = x.shape[1] // 2
    bits = lax.bitcast_convert_type(x.astype(BF16).astype(F32), U32)
    return bits[:, :w] | (bits[:, w:] >> 16)


def _unpack_bf16_pair(p):
    hi = lax.bitcast_convert_type(p & jnp.uint32(0xFFFF0000), F32)
    lo = lax.bitcast_convert_type(p << 16, F32)
    return jnp.concatenate([hi, lo], axis=-1)


def _silu(x):
    return x * jax.nn.sigmoid(x)


def _rms(x):
    return x * lax.rsqrt(jnp.mean(x * x, axis=-1, keepdims=True) + EPS)


def _group_rms_scale(x, gmat, emat):
    ms = _dot((x * x).astype(BF16), gmat)
    r = lax.rsqrt(ms + EPS)
    r_hi, r_lo = _split2(r)
    return x * (_dot(r_hi, emat) + _dot(r_lo, emat))


def _rope(x, cos, sin_a, sin_b, quarter):
    w = x.shape[-1]
    rep = lambda t: t if t.shape[-1] == w else jnp.concatenate([t] * (w // t.shape[-1]), axis=-1)
    return x * rep(cos) + pltpu.roll(x, quarter, 1) * rep(sin_a) + pltpu.roll(x, w - quarter, 1) * rep(sin_b)


def _const_spec(shape):
    nd = len(shape)
    return pl.BlockSpec(shape, lambda *_: (0,) * nd, pipeline_mode=pl.Buffered(1))


def _mod_kernel(c_ref, w_ref, b_ref, o_ref):
    s = _silu(c_ref[...])
    o_ref[0] = jnp.dot(s, w_ref[0], precision=lax.Precision.HIGHEST, preferred_element_type=F32) + b_ref[0]


def _modulation(cc, ada_w, ada_b):
    depth, d, n = ada_w.shape
    tn = 1536
    return pl.pallas_call(
        _mod_kernel,
        grid=(depth, n // tn),
        in_specs=[pl.BlockSpec((MOD_ROWS, d), lambda l, j: (0, 0)),
                  pl.BlockSpec((1, d, tn), lambda l, j: (l, 0, j)),
                  pl.BlockSpec((1, 1, tn), lambda l, j: (l, 0, j))],
        out_specs=pl.BlockSpec((1, MOD_ROWS, tn), lambda l, j: (l, 0, j)),
        out_shape=jax.ShapeDtypeStruct((depth, MOD_ROWS, n), F32),
        compiler_params=_cparams(("arbitrary", "arbitrary")),
        name="adaln_mod",
    )(cc, ada_w, ada_b.reshape(depth, 1, n))


def _inproj_kernel(x_ref, mod_ref, g1_ref, w_ref, wuq_ref, wuk_ref, wuv_ref,
                   gd_ref, ed_ref, gq_ref, eq_ref, gk_ref, ek_ref, ekr_ref,
                   rope_ref, dqg_ref, dkg_ref, cqg_ref, ckvg_ref, mqg_ref, mkg_ref, krg_ref, mvo_ref,
                   dproj_ref, xbc_ref, dt_ref, mproj_ref, gates_ref, *, n_batch):
    t = pl.program_id(0)
    b = pl.program_id(1)
    row = jnp.where(t == 0, n_batch, b)
    shift = mod_ref[pl.ds(row, 1), 0:D_MODEL]
    scale = mod_ref[pl.ds(row, 1), D_MODEL:2 * D_MODEL]
    xn = _rms(x_ref[0]) * g1_ref[...]
    h = (xn * (1.0 + scale) + shift).astype(BF16)

    big = _dot(h, w_ref[...])

    def proj(seg):
        return big[:, seg[0]:seg[1]]

    cosd, sad, sbd, cosq, saq, sbq = (rope_ref[:, i * LANES:(i + 1) * LANES] for i in range(6))
    dq = _group_rms_scale(proj(SEG_DQ), gd_ref[...], ed_ref[...]) * dqg_ref[...]
    dproj_ref[0, :, 0:512] = _rope(dq, cosd, sad, sbd, DIFF_DH // 4).astype(BF16)
    dk = _group_rms_scale(proj(SEG_DK), gd_ref[...], ed_ref[...]) * dkg_ref[...]
    dproj_ref[0, :, 512:1024] = _rope(dk, cosd, sad, sbd, DIFF_DH // 4).astype(BF16)
    dproj_ref[0, :, 1024:1536] = proj(SEG_DV).astype(BF16)
    dproj_ref[0, :, 1536:2048] = proj(SEG_Z).astype(BF16)
    xbc_ref[0] = proj(SEG_XBC)
    dt_ref[0] = proj(SEG_DT)
    gates_ref[0] = jax.nn.sigmoid(proj(SEG_GATES)).astype(BF16)

    cq = _rms(proj(SEG_CQ)) * cqg_ref[...]
    q = _dot(cq.astype(BF16), wuq_ref[...])
    q = _group_rms_scale(q, gq_ref[...], eq_ref[...]) * mqg_ref[...]
    wm = MLA_HEADS * MLA_HP
    mproj_ref[0, :, 0:wm] = _rope(q, cosq, saq, sbq, MLA_ROPE // 4).astype(BF16)

    ckv = (_rms(proj(SEG_CKV)) * ckvg_ref[...]).astype(BF16)
    kn = _dot(ckv, wuk_ref[...])
    kn = _group_rms_scale(kn, gk_ref[...], ek_ref[...]) * mkg_ref[...]
    mproj_ref[0, :, 2 * wm:3 * wm] = (_dot(ckv, wuv_ref[...]) + mvo_ref[...]).astype(BF16)
    kr = proj(SEG_KR)
    kr = kr * lax.rsqrt(jnp.sum(kr * kr, axis=-1, keepdims=True) * (1.0 / MLA_ROPE) + EPS) * krg_ref[...]
    kr = _rope(kr, cosq, saq, sbq, MLA_ROPE // 4).astype(BF16)
    mproj_ref[0, :, wm:2 * wm] = (kn + _dot(kr, ekr_ref[...])).astype(BF16)


def _rope_tables(n_ctx, n_lat, dim, lane_layout):
    quarter = dim // 4
    inv_freq = ROPE_BASE ** (-jnp.arange(quarter, dtype=F32) / quarter)
    s = jnp.arange(n_lat, dtype=I32)
    ar = (s // GRID_W).astype(F32)[:, None] * inv_freq
    ac = (s % GRID_W).astype(F32)[:, None] * inv_freq
    ang = jnp.concatenate([ar, ar, ac, ac], axis=-1)
    ang = jnp.concatenate([jnp.zeros((n_ctx, dim), F32), ang], axis=0)
    odd = jnp.asarray((np.arange(dim) // quarter) % 2 == 1)
    cos_d, sin_d = jnp.cos(ang), jnp.sin(ang)
    units = (cos_d, jnp.where(odd, sin_d, 0.0), jnp.where(odd, 0.0, -sin_d))
    out = []
    for unit, fill in zip(units, (1.0, 0.0, 0.0)):
        pieces = [unit if item == 'rope' else jnp.full((n_ctx + n_lat, item), fill, F32) for item in lane_layout]
        out.append(jnp.concatenate(pieces, axis=-1))
    return out


def _group_mats(width, groups):
    g = np.zeros((width, LANES), np.float32)
    e = np.zeros((LANES, width), np.float32)
    for i, (st, sz) in enumerate(groups):
        g[st:st + sz, i] = 1.0 / sz
        e[i, st:st + sz] = 1.0
    return jnp.asarray(g, BF16), jnp.asarray(e, BF16)


def _static_tables(n_ctx, n_lat):
    tabs = {}
    tabs['gd'], tabs['ed'] = _group_mats(512, [(i * DIFF_DH, DIFF_DH) for i in range(2 * DIFF_HEADS)])
    qgroups = []
    for h in range(MLA_HEADS):
        qgroups += [(h * MLA_HP, MLA_NOPE), (h * MLA_HP + MLA_NOPE, MLA_ROPE)]
    tabs['gq'], tabs['eq'] = _group_mats(MLA_HEADS * MLA_HP, qgroups)
    tabs['gk'], tabs['ek'] = _group_mats(MLA_HEADS * MLA_HP, [(h * MLA_HP, MLA_NOPE) for h in range(MLA_HEADS)])
    ekr = np.zeros((LANES, MLA_HEADS * MLA_HP), np.float32)
    for h in range(MLA_HEADS):
        for j in range(MLA_ROPE):
            ekr[MLA_NOPE + j, h * MLA_HP + MLA_NOPE + j] = 1.0
    tabs['ekr'] = jnp.asarray(ekr, BF16)
    mv_ones = np.zeros((1, MLA_HEADS * MLA_HP), np.float32)
    for h in range(MLA_HEADS):
        mv_ones[0, h * MLA_HP + (MLA_V if h % 2 == 0 else 0)] = 1.0
    tabs['mv_ones'] = jnp.asarray(mv_ones)
    tabs['rope'] = jnp.concatenate(
        _rope_tables(n_ctx, n_lat, DIFF_DH, ['rope'] * (LANES // DIFF_DH))
        + _rope_tables(n_ctx, n_lat, MLA_ROPE, [MLA_NOPE, 'rope', MLA_HP - MLA_QK]), axis=-1)
    return tabs


def _in_projection(xs, mod_l, p, tabs):
    n_batch, n_rows, d = xs.shape
    nt = n_rows // TM
    row_spec = lambda w: pl.BlockSpec((1, TM, w), lambda t, b: (b, t, 0))
    tab_spec = lambda w: pl.BlockSpec((TM, w), lambda t, b: (t, 0))
    consts = [mod_l, p['g1'], p['w_in'], p['w_uq'], p['w_uk'], p['w_uv'],
              tabs['gd'], tabs['ed'], tabs['gq'], tabs['eq'], tabs['gk'], tabs['ek'], tabs['ekr']]
    rope = [tabs['rope']]
    gains = [p['dq_g'], p['dk_g'], p['cq_g'], p['ckv_g'], p['mq_g'], p['mk_g'], p['kr_g'], tabs['mv_ones']]
    out_w = [(4 * 512, BF16), (1024, F32), (LANES, F32), (3 * MLA_HEADS * MLA_HP, BF16), (N_BRANCH * D_MODEL, BF16)]
    return pl.pallas_call(
        functools.partial(_inproj_kernel, n_batch=n_batch),
        grid=(nt, n_batch),
        in_specs=([row_spec(d)] + [_const_spec(a.shape) for a in consts]
                  + [tab_spec(a.shape[1]) for a in rope] + [_const_spec(a.shape) for a in gains]),
        out_specs=[row_spec(w) for w, _ in out_w],
        out_shape=[jax.ShapeDtypeStruct((n_batch, n_rows, w), dt) for w, dt in out_w],
        compiler_params=_cparams(("arbitrary", "arbitrary")),
        name="in_proj",
    )(xs, *consts, *rope, *gains)


def _ssd_chunk_of_step(s, direction, n_ctx_chunks, n_chunks):
    if direction == 0:
        return s
    return jnp.where(s < n_ctx_chunks, n_ctx_chunks - 1 - s, n_chunks - 1 + n_ctx_chunks - s)


def _ssd_kernel(*refs, direction, n_ctx_chunks, n_chunks):
    if direction == 0:
        (xbc_ref, prev_ref, next_ref, dt_ref, cw_ref, cb_ref, dtb_ref, alog_ref, y_ref, u_ref, state_ref) = refs
    else:
        (u_ref, dt_ref, dtb_ref, alog_ref, yf_ref, z_ref, dskip_ref, ng_ref, y_ref, state_ref) = refs
    s = pl.program_id(1)
    c = _ssd_chunk_of_step(s, direction, n_ctx_chunks, n_chunks)
    ck = SSD_CHUNK

    @pl.when(s == 0)
    def _():
        state_ref[...] = jnp.zeros_like(state_ref)

    if direction == 0:
        first = jnp.logical_or(c == 0, c == n_ctx_chunks)
        last = jnp.logical_or(c == n_ctx_chunks - 1, c == n_chunks - 1)
        prev = jnp.where(first, 0.0, prev_ref[0])
        nxt = jnp.where(last, 0.0, next_ref[0])
        full = jnp.concatenate([prev, xbc_ref[0], nxt], axis=0)
        n_full = ck + 2 * HALO
        half = (SSD_CONV - 1) // 2
        conv = None
        for k in range(SSD_CONV):
            sh = (half - k) % n_full
            rolled = full if sh == 0 else pltpu.roll(full, sh, 0)
            term = rolled[HALO:HALO + ck] * cw_ref[k:k + 1, :]
            conv = term if conv is None else conv + term
        u = _silu(conv + cb_ref[...])
        u_ref[0] = u.astype(BF16)
    else:
        u = u_ref[0].astype(F32)
    xs = u[:, :SSD_INNER]
    bm = u[:, SSD_INNER:SSD_INNER + SSD_GROUPS * SSD_STATE]
    cm = u[:, SSD_INNER + SSD_GROUPS * SSD_STATE:]

    dt = jax.nn.softplus(dt_ref[0] + dtb_ref[...])
    a = dt * (-jnp.exp(alog_ref[...]))
    ri = lax.broadcasted_iota(I32, (ck, ck), 0)
    ci = lax.broadcasted_iota(I32, (ck, ck), 1)
    valid = (ci <= ri) if direction == 0 else (ci >= ri)
    tri = jnp.where(valid, 1.0, 0.0).astype(BF16)
    a1, a2, a3 = _split3(a)
    acum = _dot(tri, a1) + _dot(tri, a2) + _dot(tri, a3)
    end_row = ck - 1 if direction == 0 else 0
    atot = acum[end_row:end_row + 1, :]
    w_end = jnp.exp(atot - acum) * dt
    e_in = jnp.exp(acum)
    cdec = jnp.exp(atot)
    acum_t = acum.T
    dt_t = dt.T
    lane = lax.broadcasted_iota(I32, (1, LANES), 1)
    left = lane < SSD_P

    pairs_per_group = SSD_HEADS // SSD_GROUPS // 2
    ys = []
    for g in range(SSD_GROUPS):
        bg = bm[:, g * SSD_STATE:(g + 1) * SSD_STATE]
        cg = cm[:, g * SSD_STATE:(g + 1) * SSD_STATE].astype(BF16)
        cb = _dot_nt(cg, bg.astype(BF16))
        bg_t = bg.T.astype(BF16)
        for pp in range(pairs_per_group):
            pr = g * pairs_per_group + pp
            xs_p = xs[:, pr * LANES:(pr + 1) * LANES]
            c0 = direction * SSD_HEADS + 2 * pr
            yd = None
            for side in range(2):
                col = c0 + side
                seg = acum[:, col:col + 1] - acum_t[col:col + 1, :]
                decay = jnp.where(valid, jnp.exp(jnp.where(valid, seg, 0.0)), 0.0)
                mix = (cb * decay * dt_t[col:col + 1, :]).astype(BF16)
                keep = left if side == 0 else jnp.logical_not(left)
                part = _dot(mix, jnp.where(keep, xs_p, 0.0).astype(BF16))
                yd = part if yd is None else yd + part
            pick = lambda m: jnp.where(left, m[:, c0:c0 + 1], m[:, c0 + 1:c0 + 2])
            h_in = state_ref[pr]
            y_off = _dot(cg, h_in.astype(BF16)) * pick(e_in)
            xw = (xs_p * pick(w_end)).astype(BF16)
            state_ref[pr] = h_in * pick(cdec) + _dot(bg_t, xw)
            ys.append(yd + y_off)
    y = jnp.concatenate(ys, axis=-1)
    if direction == 0:
        y_ref[0] = y
    else:
        y = y + yf_ref[0] + dskip_ref[...] * xs
        y = y * _silu(z_ref[0].astype(F32))
        y_ref[0] = (_rms(y) * ng_ref[...]).astype(BF16)


def _ssd_scan(xbc, dt, p, n_ctx, direction, yf=None, z=None):
    n_batch, n_rows, wc = xbc.shape
    ck = SSD_CHUNK
    n_chunks = n_rows // ck
    n_ctx_chunks = n_ctx // ck
    n_halo_blocks = n_rows // HALO
    per = ck // HALO
    cmap = lambda s: _ssd_chunk_of_step(s, direction, n_ctx_chunks, n_chunks)
    cur = lambda w: pl.BlockSpec((1, ck, w), lambda b, s: (b, cmap(s), 0))
    if direction == 0:
        in_specs = [cur(wc),
                    pl.BlockSpec((1, HALO, wc), lambda b, s: (b, jnp.maximum(cmap(s) * per - 1, 0), 0)),
                    pl.BlockSpec((1, HALO, wc),
                                 lambda b, s: (b, jnp.minimum((cmap(s) + 1) * per, n_halo_blocks - 1), 0)),
                    cur(LANES),
                    _const_spec(p['conv_w'].shape), _const_spec(p['conv_b'].shape),
                    _const_spec(p['dt_bias'].shape), _const_spec(p['a_log'].shape)]
        args = [xbc, xbc, xbc, dt, p['conv_w'], p['conv_b'], p['dt_bias'], p['a_log']]
        out_specs = [cur(SSD_INNER), cur(wc)]
        out_shape = [jax.ShapeDtypeStruct((n_batch, n_rows, SSD_INNER), F32),
                     jax.ShapeDtypeStruct((n_batch, n_rows, wc), BF16)]
    else:
        z_spec = pl.BlockSpec((1, ck, SSD_INNER), lambda b, s: (b, cmap(s), 3))
        in_specs = [cur(wc), cur(LANES), _const_spec(p['dt_bias'].shape), _const_spec(p['a_log'].shape),
                    cur(SSD_INNER), z_spec, _const_spec(p['d_skip'].shape), _const_spec(p['ssd_g'].shape)]
        args = [xbc, dt, p['dt_bias'], p['a_log'], yf, z, p['d_skip'], p['ssd_g']]
        out_specs = cur(SSD_INNER)
        out_shape = jax.ShapeDtypeStruct((n_batch, n_rows, SSD_INNER), BF16)
    return pl.pallas_call(
        functools.partial(_ssd_kernel, direction=direction, n_ctx_chunks=n_ctx_chunks, n_chunks=n_chunks),
        grid=(n_batch, n_chunks),
        in_specs=in_specs,
        out_specs=out_specs,
        out_shape=out_shape,
        scratch_shapes=[pltpu.VMEM((SSD_HEADS // 2, SSD_STATE, LANES), F32)],
        compiler_params=_cparams(("arbitrary", "arbitrary")),
        name="ssd_fwd" if direction == 0 else "ssd_bwd",
    )(*args)


def _softmax_parts(s2):
    m = jnp.max(s2, axis=-1, keepdims=True)
    e = jnp.exp2(s2 - m)
    return e, jnp.sum(e, axis=-1, keepdims=True)


def _softmax_numerator(s2):
    m = jnp.max(s2, axis=-1, keepdims=True)
    return jnp.exp2((s2 - m).astype(BF16))


def _diff_attn_kernel(q_ref, k_ref, v_ref, lam_ref, sg_ref, o_ref, *, n_ctx, lam_init, t0):
    lv = lam_ref[...]
    lam = (jnp.exp(jnp.sum(lv[0:1] * lv[1:2], axis=-1, keepdims=True))
           - jnp.exp(jnp.sum(lv[2:3] * lv[3:4], axis=-1, keepdims=True)) + lam_init)
    lane = lax.broadcasted_iota(I32, (1, DIFF_VD), 1)

    def attend(n_keys):
        for hd in range(DIFF_HEADS_PER_STEP):
            sl = slice(hd * DIFF_VD, (hd + 1) * DIFF_VD)
            q = q_ref[0, :, sl]
            q0 = jnp.where(lane < DIFF_DH, q, jnp.zeros_like(q))
            q1 = jnp.where(lane < DIFF_DH, jnp.zeros_like(q), q)
            k = k_ref[0, 0:n_keys, sl]
            e0, l0 = _softmax_parts(_dot_nt(q0, k))
            e1, l1 = _softmax_parts(_dot_nt(q1, k))
            w = (e0 - (lam * l0 / l1) * e1).astype(BF16)
            o = _dot(w, v_ref[0, 0:n_keys, sl]) * (1.0 / l0)
            o_ref[0, :, sl] = (_rms(o) * sg_ref[...]).astype(BF16)

    is_ctx = t0 + pl.program_id(2) == 0

    @pl.when(is_ctx)
    def _():
        attend(n_ctx)

    @pl.when(jnp.logical_not(is_ctx))
    def _():
        attend(k_ref.shape[1])


def _diff_attention(dproj, lam_p, subln_g, n_ctx, lam_init, t0):
    n_batch, n_rows, _ = dproj.shape
    nq = n_rows // TM - t0
    nh = DIFF_HEADS // DIFF_HEADS_PER_STEP
    return pl.pallas_call(
        functools.partial(_diff_attn_kernel, n_ctx=n_ctx, lam_init=lam_init, t0=t0),
        grid=(n_batch, nh, nq),
        in_specs=[pl.BlockSpec((1, TM, DIFF_HEADS_PER_STEP * DIFF_VD), lambda b, h, t: (b, t0 + t, h)),
                  pl.BlockSpec((1, n_rows, DIFF_HEADS_PER_STEP * DIFF_VD), lambda b, h, t: (b, 0, nh + h)),
                  pl.BlockSpec((1, n_rows, DIFF_HEADS_PER_STEP * DIFF_VD), lambda b, h, t: (b, 0, 2 * nh + h)),
                  _const_spec(lam_p.shape), _const_spec(subln_g.shape)],
        out_specs=pl.BlockSpec((1, TM, DIFF_HEADS_PER_STEP * DIFF_VD), lambda b, h, t: (b, t, h)),
        out_shape=jax.ShapeDtypeStruct((n_batch, nq * TM, DIFF_HEADS * DIFF_VD), BF16),
        compiler_params=_cparams(("arbitrary", "arbitrary", "arbitrary")),
        name="diff_attn",
    )(dproj, dproj, dproj, lam_p, subln_g)


def _mla_attn_kernel(q_ref, k_ref, v_ref, o_ref, *, n_ctx, t0):
    lane = lax.broadcasted_iota(I32, (1, MLA_HP), 1)

    def attend(n_keys):
        res = []
        for hd in range(MLA_HEADS_PER_STEP):
            sl = slice(hd * MLA_HP, (hd + 1) * MLA_HP)
            e = _softmax_numerator(_dot_nt(q_ref[0, :, sl], k_ref[0, 0:n_keys, sl]))
            res.append(_dot(e, v_ref[0, 0:n_keys, sl]))
        for pr in range(MLA_HEADS_PER_STEP // 2):
            r0, r1 = res[2 * pr], res[2 * pr + 1]
            o0 = r0 * (1.0 / r0[:, MLA_V:MLA_V + 1])
            o1 = r1 * (1.0 / r1[:, 0:1])
            o_ref[0, :, pr * MLA_HP:(pr + 1) * MLA_HP] = jnp.where(lane < MLA_V, o0, o1).astype(BF16)

    is_ctx = t0 + pl.program_id(2) == 0

    @pl.when(is_ctx)
    def _():
        attend(n_ctx)

    @pl.when(jnp.logical_not(is_ctx))
    def _():
        attend(k_ref.shape[1])


def _mla_attention(mproj, n_ctx, t0):
    n_batch, n_rows, _ = mproj.shape
    nq = n_rows // TM - t0
    nh = MLA_HEADS // MLA_HEADS_PER_STEP
    return pl.pallas_call(
        functools.partial(_mla_attn_kernel, n_ctx=n_ctx, t0=t0),
        grid=(n_batch, nh, nq),
        in_specs=[pl.BlockSpec((1, TM, MLA_HEADS_PER_STEP * MLA_HP), lambda b, h, t: (b, t0 + t, h)),
                  pl.BlockSpec((1, n_rows, MLA_HEADS_PER_STEP * MLA_HP), lambda b, h, t: (b, 0, nh + h)),
                  pl.BlockSpec((1, n_rows, MLA_HEADS_PER_STEP * MLA_HP), lambda b, h, t: (b, 0, 2 * nh + h))],
        out_specs=pl.BlockSpec((1, TM, MLA_HEADS_PER_STEP * MLA_V), lambda b, h, t: (b, t, h)),
        out_shape=jax.ShapeDtypeStruct((n_batch, nq * TM, MLA_HEADS * MLA_V), BF16),
        compiler_params=_cparams(("arbitrary", "arbitrary", "arbitrary")),
        name="mla_attn",
    )(mproj, mproj, mproj)


def _merge_kernel(x_ref, da_ref, ss_ref, ma_ref, gates_ref, mod_ref, wb_ref, wo_ref, g2_ref,
                  rwh_ref, rwl_ref, rb_ref,
                  xo_ref, h2_ref, route_ref, cnt_ref, base_ref, *, n_batch, t0):
    b = pl.program_id(0)
    t = pl.program_id(1)

    @pl.when(jnp.logical_and(b == 0, t == 0))
    def _():
        base_ref[...] = jnp.zeros_like(base_ref)

    row = jnp.where(t0 + t == 0, n_batch, b)
    gate1 = mod_ref[pl.ds(row, 1), 2 * D_MODEL:3 * D_MODEL]
    shift2 = mod_ref[pl.ds(row, 1), 3 * D_MODEL:4 * D_MODEL]
    scale2 = mod_ref[pl.ds(row, 1), 4 * D_MODEL:5 * D_MODEL]

    acc = None
    for k, o_ref in enumerate((da_ref, ss_ref, ma_ref)):
        term = gates_ref[0, :, k * D_MODEL:(k + 1) * D_MODEL].astype(F32) * _dot(o_ref[0], wb_ref[k])
        acc = term if acc is None else acc + term
    x = x_ref[0] + gate1 * _dot(acc.astype(BF16), wo_ref[...])
    xo_ref[0] = x
    h2 = _rms(x) * g2_ref[...] * (1.0 + scale2) + shift2
    h2_ref[0] = _pack_bf16_pair(h2)

    h_hi, h_lo = _split2(h2)
    lg = _dot(h_hi, rwh_ref[...]) + _dot(h_lo, rwh_ref[...]) + _dot(h_hi, rwl_ref[...]) + rb_ref[...]
    lane = lax.broadcasted_iota(I32, lg.shape, 1)
    lane_f = lane.astype(F32)
    big = float(LANES)
    is_g = lane < MOE_GROUPS
    gl = jnp.where(is_g, lg, NEG)
    gmax = jnp.max(gl, axis=-1, keepdims=True)
    gidx = jnp.min(jnp.where(gl == gmax, lane_f, big), axis=-1, keepdims=True)
    pg = 1.0 / jnp.sum(jnp.where(is_g, jnp.exp(gl - gmax), 0.0), axis=-1, keepdims=True)
    lo_lane = MOE_GROUPS + gidx * MOE_PER_GROUP
    in_grp = jnp.logical_and(lane_f >= lo_lane, lane_f < lo_lane + MOE_PER_GROUP)
    el = jnp.where(in_grp, lg, NEG)
    m1 = jnp.max(el, axis=-1, keepdims=True)
    i1 = jnp.min(jnp.where(el == m1, lane_f, big), axis=-1, keepdims=True)
    el2 = jnp.where(lane_f == i1, NEG, el)
    m2 = jnp.max(el2, axis=-1, keepdims=True)
    i2 = jnp.min(jnp.where(el2 == m2, lane_f, big), axis=-1, keepdims=True)
    tt = jnp.exp(m2 - m1)
    w1 = pg / (1.0 + tt)
    w2 = pg * tt / (1.0 + tt)
    e1 = i1 - MOE_GROUPS
    e2 = i2 - MOE_GROUPS

    oh1 = lane_f == e1
    oh2 = lane_f == e2
    oh = jnp.where(jnp.logical_or(oh1, oh2), 1.0, 0.0)
    tm = oh.shape[0]
    ri = lax.broadcasted_iota(I32, (tm, tm), 0)
    ci = lax.broadcasted_iota(I32, (tm, tm), 1)
    strict = jnp.where(ci < ri, 1.0, 0.0).astype(BF16)
    base = base_ref[0:1, :]
    rank_all = _dot(strict, oh.astype(BF16)) + base
    r1 = jnp.sum(jnp.where(oh1, rank_all, 0.0), axis=-1, keepdims=True)
    r2 = jnp.sum(jnp.where(oh2, rank_all, 0.0), axis=-1, keepdims=True)
    new_base = base + jnp.sum(oh, axis=0, keepdims=True)
    base_ref[...] = jnp.broadcast_to(new_base, base_ref.shape)
    cnt_ref[...] = jnp.broadcast_to(new_base, cnt_ref.shape)

    rec = jnp.zeros(lg.shape, F32)
    for ln, val in ((R_E1, e1), (R_E2, e2), (R_R1, r1), (R_R2, r2), (R_W1, w1), (R_W2, w2)):
        rec = jnp.where(lane == ln, val, rec)
    route_ref[0] = rec


def _merge(xs, da, ss, ma, gates, mod_l, p, t0):
    n_batch, n_rows, d = xs.shape
    nt = n_rows // TM - t0
    rows_out = nt * TM
    in_row = lambda w: pl.BlockSpec((1, TM, w), lambda b, t: (b, t0 + t, 0))
    out_row = lambda w: pl.BlockSpec((1, TM, w), lambda b, t: (b, t, 0))
    consts = [mod_l, p['w_branch'], p['w_out'], p['g2'], p['rw_hi'], p['rw_lo'], p['rb']]
    return pl.pallas_call(
        functools.partial(_merge_kernel, n_batch=n_batch, t0=t0),
        grid=(n_batch, nt),
        in_specs=[in_row(d), out_row(512), in_row(512), out_row(512), in_row(N_BRANCH * D_MODEL)]
                 + [_const_spec(a.shape) for a in consts],
        out_specs=[out_row(d), out_row(d // 2), out_row(LANES), pl.BlockSpec((SUBLANES, LANES), lambda b, t: (0, 0))],
        out_shape=[jax.ShapeDtypeStruct((n_batch, rows_out, d), F32),
                   jax.ShapeDtypeStruct((n_batch, rows_out, d // 2), U32),
                   jax.ShapeDtypeStruct((n_batch, rows_out, LANES), F32),
                   jax.ShapeDtypeStruct((SUBLANES, LANES), F32)],
        scratch_shapes=[pltpu.VMEM((SUBLANES, LANES), F32)],
        compiler_params=_cparams(("arbitrary", "arbitrary")),
        name="merge_route",
    )(xs, da, ss, ma, gates, *consts)


def _row_copy(src_ref, src_row, dst_ref, dst_row, sem):
    return pltpu.make_async_copy(src_ref.at[pl.ds(src_row, 1), :], dst_ref.at[pl.ds(dst_row, 1), :], sem)


def _dispatch_kernel(seg_ref, slots_ref, h2_ref, xs_ref, zero_ref, sem, zsem):
    tm = h2_ref.shape[1]
    src = h2_ref.at[0]

    @pl.when(jnp.logical_and(pl.program_id(0) == 0, pl.program_id(1) == 0))
    def _():
        zero_ref[...] = jnp.zeros_like(zero_ref)

        def zero_tail(e, carry):
            start = seg_ref[e]
            end = seg_ref[e + 1]

            @pl.when(end > start)
            def _():
                cp = pltpu.make_async_copy(
                    zero_ref, xs_ref.at[pl.ds(pl.multiple_of(end - MOE_BM, MOE_BM), MOE_BM), :], zsem)
                cp.start()
                cp.wait()
            return carry

        lax.fori_loop(0, MOE_EXPERTS, zero_tail, 0)

        def zero_unused(j, carry):
            cp = pltpu.make_async_copy(zero_ref, xs_ref.at[pl.ds(pl.multiple_of(j * MOE_BM, MOE_BM), MOE_BM), :], zsem)
            cp.start()
            cp.wait()
            return carry

        lax.fori_loop(seg_ref[MOE_EXPERTS] // MOE_BM, xs_ref.shape[0] // MOE_BM, zero_unused, 0)

    def issue(r, carry):
        for which in range(MOE_TOP_K):
            _row_copy(src, r, xs_ref, slots_ref[0, 0, which * tm + r], sem).start(priority=which)
        return carry

    lax.fori_loop(0, tm, issue, 0, unroll=ROW_DMA_UNROLL)
    for _ in range(MOE_TOP_K):
        pltpu.make_async_copy(src, xs_ref.at[pl.ds(0, tm), :], sem).wait()


def _dispatch(h2, seg_start, slots, n_slots):
    n_batch, n_rows, d = h2.shape
    nt = n_rows // TM
    grid_spec = pltpu.PrefetchScalarGridSpec(
        num_scalar_prefetch=1,
        grid=(n_batch, nt),
        in_specs=[pl.BlockSpec((1, 1, MOE_TOP_K * TM), lambda b, t, seg: (b * nt + t, 0, 0),
                               memory_space=pltpu.SMEM),
                  pl.BlockSpec((1, TM, d), lambda b, t, seg: (b, t, 0))],
        out_specs=pl.BlockSpec(memory_space=pl.ANY),
        scratch_shapes=[pltpu.VMEM((MOE_BM, d), U32), pltpu.SemaphoreType.DMA(()), pltpu.SemaphoreType.DMA(())],
    )
    return pl.pallas_call(
        _dispatch_kernel,
        grid_spec=grid_spec,
        out_shape=jax.ShapeDtypeStruct((n_slots, d), U32),
        compiler_params=_cparams(("arbitrary", "arbitrary")),
        name="moe_dispatch",
    )(seg_start, slots, h2)


def _expert_kernel(be_ref, nu_ref, xs_ref, wgu_ref, wd_ref, y_ref):
    i = pl.program_id(0)

    @pl.when(i < nu_ref[0])
    def _():
        gu = _dot(_unpack_bf16_pair(xs_ref[...]).astype(BF16), wgu_ref[0])
        hid = _silu(gu[:, :MOE_HIDDEN]) * gu[:, MOE_HIDDEN:]
        y_ref[...] = _pack_bf16_pair(_dot(hid.astype(BF16), wd_ref[0]))

    @pl.when(i >= nu_ref[0])
    def _():
        y_ref[...] = jnp.zeros_like(y_ref)


def _expert_ffn(xs, blk_expert, n_used, w_gu, w_d):
    n_slots, dp = xs.shape
    n_blocks = n_slots // MOE_BM
    grid_spec = pltpu.PrefetchScalarGridSpec(
        num_scalar_prefetch=2,
        grid=(n_blocks,),
        in_specs=[pl.BlockSpec((MOE_BM, dp), lambda i, be, nu: (jnp.minimum(i, nu[0] - 1), 0)),
                  pl.BlockSpec((1, D_MODEL, 2 * MOE_HIDDEN), lambda i, be, nu: (be[i], 0, 0)),
                  pl.BlockSpec((1, MOE_HIDDEN, D_MODEL), lambda i, be, nu: (be[i], 0, 0))],
        out_specs=pl.BlockSpec((MOE_BM, dp), lambda i, be, nu: (i, 0)),
    )
    return pl.pallas_call(
        _expert_kernel,
        grid_spec=grid_spec,
        out_shape=jax.ShapeDtypeStruct((n_slots, dp), U32),
        compiler_params=_cparams(("arbitrary",)),
        name="moe_experts",
    )(blk_expert, n_used, xs, w_gu, w_d)


def _combine_kernel(slots_ref, x_ref, route_ref, mod_ref, y_ref, o_ref, buf_ref, sem, *, n_batch, t0):
    b = pl.program_id(0)
    t = pl.program_id(1)
    tm = x_ref.shape[1]

    def issue(r, carry):
        for which in range(MOE_TOP_K):
            _row_copy(y_ref, slots_ref[0, 0, which * tm + r], buf_ref.at[which], r, sem).start(priority=which)
        return carry

    lax.fori_loop(0, tm, issue, 0, unroll=ROW_DMA_UNROLL)
    for which in range(MOE_TOP_K):
        pltpu.make_async_copy(y_ref.at[pl.ds(0, tm), :], buf_ref.at[which], sem).wait()

    row = jnp.where(t0 + t == 0, n_batch, b)
    gate2 = mod_ref[pl.ds(row, 1), 5 * D_MODEL:6 * D_MODEL]
    rec = route_ref[0]
    f = (rec[:, R_W1:R_W1 + 1] * _unpack_bf16_pair(buf_ref[0])
         + rec[:, R_W2:R_W2 + 1] * _unpack_bf16_pair(buf_ref[1]))
    o_ref[0] = x_ref[0] + gate2 * f


def _combine(x_mid, route, slots, y, mod_l, t0):
    n_batch, n_rows, d = x_mid.shape
    nt = n_rows // TM
    row = lambda w: pl.BlockSpec((1, TM, w), lambda b, t: (b, t, 0))
    return pl.pallas_call(
        functools.partial(_combine_kernel, n_batch=n_batch, t0=t0),
        grid=(n_batch, nt),
        in_specs=[pl.BlockSpec((1, 1, MOE_TOP_K * TM), lambda b, t: (b * nt + t, 0, 0), memory_space=pltpu.SMEM),
                  row(d), row(LANES), _const_spec(mod_l.shape),
                  pl.BlockSpec(memory_space=pl.ANY)],
        out_specs=row(d),
        out_shape=jax.ShapeDtypeStruct((n_batch, n_rows, d), F32),
        scratch_shapes=[pltpu.VMEM((MOE_TOP_K, TM, d // 2), U32), pltpu.SemaphoreType.DMA(())],
        compiler_params=_cparams(("arbitrary", "arbitrary")),
        name="moe_combine",
    )(slots, x_mid, route, mod_l, y)


def _moe(x_mid, h2, route, counts, mod_l, p, t0):
    n_batch, n_rows, d = h2.shape
    n_tok = n_batch * n_rows
    n_blocks = -(-(n_tok * MOE_TOP_K) // MOE_BM) + MOE_EXPERTS
    cnt = counts[0, :MOE_EXPERTS].astype(I32)
    padded = (cnt + MOE_BM - 1) // MOE_BM * MOE_BM
    pad_end = jnp.cumsum(padded)
    seg_start = jnp.concatenate([jnp.zeros((1,), I32), pad_end]).astype(I32)
    blk_first = jnp.arange(n_blocks, dtype=I32) * MOE_BM
    blk_expert = jnp.minimum(jnp.sum((pad_end[None, :] <= blk_first[:, None]).astype(I32), axis=1), MOE_EXPERTS - 1)
    n_used = (pad_end[-1:] // MOE_BM).astype(I32)
    e = route[..., R_E1:R_E2 + 1].astype(I32)
    start = jnp.sum(jnp.where(e[..., None] == jnp.arange(MOE_EXPERTS, dtype=I32), seg_start[:MOE_EXPERTS], 0), axis=-1)
    slots = (start + route[..., R_R1:R_R2 + 1].astype(I32)).reshape(n_tok // TM, TM, MOE_TOP_K)
    slots = jnp.swapaxes(slots, 1, 2).reshape(n_tok // TM, 1, MOE_TOP_K * TM)
    xs = _dispatch(h2, seg_start, slots, n_blocks * MOE_BM)
    y = _expert_ffn(xs, blk_expert, n_used, p['w_gu'], p['w_d'])
    return _combine(x_mid, route, slots, y, mod_l, t0)


def _prep_layer(l, w_in, diff_q_g, diff_k_g, diff_lambda, diff_subln_g, ssd_conv_w, ssd_conv_b, ssd_dt_bias,
                ssd_A_log, ssd_D, ssd_norm_g, mla_cq_g, mla_ckv_g, w_uq, w_ukv, mla_q_g, mla_k_g, w_branch, w_out,
                norm1_g, norm2_g, moe_group_w, moe_group_b, moe_expert_w, moe_expert_b, moe_w_gate, moe_w_up,
                moe_w_down):
    d = D_MODEL
    lam_init = 0.8 - 0.6 * math.exp(-0.3 * l)
    wi = w_in[l]
    sizes = (512, 512, 512, 512, 1024, 16, MLA_Q_LORA, MLA_KV_LORA, MLA_ROPE, N_BRANCH * d)
    offs = np.concatenate([[0], np.cumsum(sizes)])
    segs = (SEG_DQ, SEG_DK, SEG_DV, SEG_Z, SEG_XBC, SEG_DT, SEG_CQ, SEG_CKV, SEG_KR, SEG_GATES)
    cols = []
    for i, (st, en) in enumerate(segs):
        piece = wi[:, offs[i]:offs[i + 1]]
        left = MLA_NOPE if (st, en) == SEG_KR else 0
        cols.append(jnp.pad(piece, ((0, 0), (left, (en - st) - sizes[i] - left))))
    p = {'w_in': jnp.concatenate(cols, axis=1).astype(BF16)}
    p['g1'] = norm1_g[l].reshape(1, d)
    p['g2'] = norm2_g[l].reshape(1, d)
    p['dq_g'] = (jnp.tile(diff_q_g[l], 2 * DIFF_HEADS) * (DIFF_DH ** -0.5 * LOG2E)).reshape(1, 512)
    p['dk_g'] = jnp.tile(diff_k_g[l], 2 * DIFF_HEADS).reshape(1, 512)
    p['cq_g'] = mla_cq_g[l].reshape(1, MLA_Q_LORA)
    p['ckv_g'] = mla_ckv_g[l].reshape(1, MLA_KV_LORA)
    zpad = jnp.zeros((MLA_HP - MLA_QK,), F32)
    p['mq_g'] = (jnp.tile(jnp.concatenate([mla_q_g[l], zpad]), MLA_HEADS) * (MLA_QK ** -0.5 * LOG2E)).reshape(1, -1)
    p['mk_g'] = jnp.tile(jnp.concatenate([mla_k_g[l][:MLA_NOPE], jnp.zeros((MLA_HP - MLA_NOPE,), F32)]),
                         MLA_HEADS).reshape(1, -1)
    p['kr_g'] = jnp.pad(mla_k_g[l][MLA_NOPE:], (MLA_NOPE, LANES - MLA_QK)).reshape(1, LANES)
    wq = w_uq[l].reshape(MLA_Q_LORA, MLA_HEADS, MLA_QK)
    p['w_uq'] = jnp.pad(wq, ((0, 0), (0, 0), (0, MLA_HP - MLA_QK))).reshape(MLA_Q_LORA, -1).astype(BF16)
    wkv = w_ukv[l].reshape(MLA_KV_LORA, MLA_HEADS, MLA_NOPE + MLA_V)
    p['w_uk'] = jnp.pad(wkv[:, :, :MLA_NOPE], ((0, 0), (0, 0), (0, MLA_HP - MLA_NOPE))).reshape(
        MLA_KV_LORA, -1).astype(BF16)
    wv = wkv[:, :, MLA_NOPE:].reshape(MLA_KV_LORA, MLA_HEADS // 2, 2, MLA_V)
    zv = jnp.zeros_like(wv[:, :, 0])
    wv = jnp.stack([jnp.concatenate([wv[:, :, 0], zv], axis=-1), jnp.concatenate([zv, wv[:, :, 1]], axis=-1)], axis=2)
    p['w_uv'] = wv.reshape(MLA_KV_LORA, MLA_HEADS * MLA_HP).astype(BF16)
    p['lam'] = diff_lambda[l]
    p['subln_g'] = (diff_subln_g[l] * (1.0 - lam_init)).reshape(1, DIFF_VD)
    p['lam_init'] = lam_init
    p['conv_w'] = jnp.pad(ssd_conv_w[l], ((0, SUBLANES - SSD_CONV), (0, 0)))
    p['conv_b'] = ssd_conv_b[l].reshape(1, -1)
    p['dt_bias'] = jnp.pad(ssd_dt_bias[l].reshape(-1), (0, LANES - 2 * SSD_HEADS)).reshape(1, LANES)
    p['a_log'] = jnp.pad(ssd_A_log[l].reshape(-1), (0, LANES - 2 * SSD_HEADS)).reshape(1, LANES)
    p['d_skip'] = jnp.repeat(ssd_D[l], SSD_P).reshape(1, SSD_INNER)
    p['ssd_g'] = ssd_norm_g[l].reshape(1, SSD_INNER)
    p['w_branch'] = w_branch[l].astype(BF16)
    p['w_out'] = w_out[l].astype(BF16)
    rw = jnp.pad(jnp.concatenate([moe_group_w[l], moe_expert_w[l]], axis=1),
                 ((0, 0), (0, LANES - MOE_GROUPS - MOE_EXPERTS)))
    p['rw_hi'], p['rw_lo'] = _split2(rw)
    p['rb'] = jnp.pad(jnp.concatenate([moe_group_b[l], moe_expert_b[l]]),
                      (0, LANES - MOE_GROUPS - MOE_EXPERTS)).reshape(1, LANES)
    p['w_gu'] = jnp.concatenate([moe_w_gate[l], moe_w_up[l]], axis=-1).astype(BF16)
    p['w_d'] = moe_w_down[l].astype(BF16)
    return p


def kernel(x, c, ctx, c_ctx, ada_w, ada_b, norm1_g, norm2_g, w_in, diff_q_g, diff_k_g, diff_lambda, diff_subln_g, ssd_conv_w, ssd_conv_b, ssd_dt_bias, ssd_A_log, ssd_D, ssd_norm_g, mla_cq_g, mla_ckv_g, w_uq, w_ukv, mla_q_g, mla_k_g, w_branch, w_out, moe_group_w, moe_group_b, moe_expert_w, moe_expert_b, moe_w_gate, moe_w_up, moe_w_down):
    n_batch, n_lat, d = x.shape
    n_ctx = ctx.shape[1]
    depth = w_in.shape[0]
    assert d == D_MODEL and n_ctx == TM and n_lat % TM == 0 and n_lat % GRID_W == 0
    assert n_batch + 1 <= MOD_ROWS

    cc = jnp.concatenate([c, c_ctx[None, :], jnp.zeros((MOD_ROWS - n_batch - 1, d), F32)], axis=0)
    mod = _modulation(cc, ada_w, ada_b)
    tabs = _static_tables(n_ctx, n_lat)
    xs = jnp.concatenate([ctx, x], axis=1)

    for l in range(depth):
        last = l == depth - 1
        t0 = 1 if last else 0
        p = _prep_layer(l, w_in, diff_q_g, diff_k_g, diff_lambda, diff_subln_g, ssd_conv_w, ssd_conv_b,
                        ssd_dt_bias, ssd_A_log, ssd_D, ssd_norm_g, mla_cq_g, mla_ckv_g, w_uq, w_ukv, mla_q_g,
                        mla_k_g, w_branch, w_out, norm1_g, norm2_g, moe_group_w, moe_group_b, moe_expert_w,
                        moe_expert_b, moe_w_gate, moe_w_up, moe_w_down)
        dproj, xbc, dt, mproj, gates = _in_projection(xs, mod[l], p, tabs)
        yf, u = _ssd_scan(xbc, dt, p, n_ctx, 0)
        ssd_o = _ssd_scan(u, dt, p, n_ctx, 1, yf=yf, z=dproj)
        diff_o = _diff_attention(dproj, p['lam'], p['subln_g'], n_ctx, p['lam_init'], t0)
        mla_o = _mla_attention(mproj, n_ctx, t0)
        x_mid, h2, route, counts = _merge(xs, diff_o, ssd_o, mla_o, gates, mod[l], p, t0)
        xs = _moe(x_mid, h2, route, counts, mod[l], p, t0)
    return xs
```
